```python
import math
import jax, jax.numpy as jnp
from jax import lax
import numpy as np


D_MODEL = 1024
BATCH = 2
SEQ = 8192
DEPTH = 1

N_MEM = 256
POOL_WINDOWS = (2, 4, 8, 16)
POOL_GROUPS = len(POOL_WINDOWS)
POOL_WIDTH = D_MODEL // 2
POOL_GROUP_DIM = POOL_WIDTH // POOL_GROUPS
DA_HEADS = 8
DA_QK_DIM = 64
DA_V_DIM = 2 * DA_QK_DIM
DA_WIDTH = DA_HEADS * DA_V_DIM
Q_BLOCK = 128
XA_HEADS = 4
XA_HEAD_DIM = D_MODEL // XA_HEADS
PEER_HEADS = 8
PEER_N_KEYS = 128
PEER_N_EXPERTS = PEER_N_KEYS * PEER_N_KEYS
PEER_QUERY_DIM = 128
PEER_HALF = PEER_QUERY_DIM // 2
PEER_TOPK = 16
PEER_TOKEN_BLOCK = 128
IN_SPLITS = (POOL_WIDTH, POOL_WIDTH + DA_WIDTH, POOL_WIDTH + 2 * DA_WIDTH,
             POOL_WIDTH + 3 * DA_WIDTH, POOL_WIDTH + 3 * DA_WIDTH + D_MODEL)
IN_WIDTH = POOL_WIDTH + 3 * DA_WIDTH + 2 * D_MODEL
DN_ALPHA = (2 * DEPTH) ** 0.25
DN_BETA = (8 * DEPTH) ** -0.25
LN_EPS = 1e-5
NEG_INF = -1e30

kernel_name = 'hybrid_pool_diffattn_peer_block'


def layer_norm(x, g, b):
    xf = x.astype(jnp.float32)
    mu = jnp.mean(xf, axis=-1, keepdims=True)
    var = jnp.mean(jnp.square(xf - mu), axis=-1, keepdims=True)
    y = (xf - mu) * lax.rsqrt(var + LN_EPS) * g.astype(jnp.float32) + b.astype(jnp.float32)
    return y.astype(x.dtype)


def alibi_slopes(n_heads):
    return jnp.exp2(-8.0 * jnp.arange(1, n_heads + 1, dtype=jnp.float32) / n_heads)


def causal_multiscale_pool(p):
    B_, S_, _ = p.shape
    pf = p.astype(jnp.float32).reshape(B_, S_, POOL_GROUPS, POOL_GROUP_DIM)
    csp = jnp.pad(lax.cumsum(pf, axis=1), ((0, 0), (1, 0), (0, 0), (0, 0)))
    pos = jnp.arange(1, S_ + 1, dtype=jnp.float32)
    outs = []
    for g, w in enumerate(POOL_WINDOWS):
        c = csp[:, :, g]
        lower = jnp.pad(c[:, :S_ - w + 1], ((0, 0), (w - 1, 0), (0, 0)))
        mean = (c[:, 1:] - lower) / jnp.minimum(pos, float(w))[None, :, None]
        outs.append(mean - pf[:, :, g])
    return jnp.stack(outs, axis=2).astype(p.dtype)


def diff_attention(q, k, v, lam, lam_init, subln_w):
    B_, S_, _ = q.shape
    nb = S_ // Q_BLOCK
    kh = k.reshape(B_, S_, DA_HEADS, 2, DA_QK_DIM)
    k1 = kh[:, :, :, 0].transpose(0, 2, 1, 3)
    k2 = kh[:, :, :, 1].transpose(0, 2, 1, 3)
    vh = v.reshape(B_, S_, DA_HEADS, DA_V_DIM).transpose(0, 2, 1, 3)
    qb = q.reshape(B_, nb, Q_BLOCK, DA_HEADS, 2, DA_QK_DIM).transpose(1, 0, 3, 4, 2, 5)
    slopes = alibi_slopes(DA_HEADS)
    kpos = jnp.arange(S_)
    scale = DA_QK_DIM ** -0.5

    def block(args):
        qblk, start = args
        qpos = start + jnp.arange(Q_BLOCK)
        dist = (qpos[:, None] - kpos[None, :]).astype(jnp.float32)
        bias = -slopes[:, None, None] * dist
        causal = dist >= 0

        def probs(qm, km):
            s = jnp.einsum('bhqd,bhkd->bhqk', qm, km).astype(jnp.float32) * scale + bias
            return jax.nn.softmax(jnp.where(causal, s, NEG_INF), axis=-1)

        a = probs(qblk[:, :, 0], k1) - lam * probs(qblk[:, :, 1], k2)
        return jnp.einsum('bhqk,bhkd->bhqd', a.astype(vh.dtype), vh)

    o = lax.map(block, (qb, jnp.arange(nb) * Q_BLOCK))
    of = o.astype(jnp.float32)
    of = of * lax.rsqrt(jnp.mean(jnp.square(of), axis=-1, keepdims=True) + LN_EPS)
    of = of * subln_w.astype(jnp.float32) * (1.0 - lam_init)
    return of.astype(v.dtype).transpose(1, 0, 3, 2, 4).reshape(B_, S_, DA_WIDTH)


def memory_cross_attention(x, mem, w_q, w_kv, w_o):
    B_, S_, _ = x.shape
    m = mem.shape[1]
    q = (x @ w_q).reshape(B_, S_, XA_HEADS, XA_HEAD_DIM)
    kv = (mem @ w_kv).reshape(B_, m, 2, XA_HEADS, XA_HEAD_DIM)
    s = jnp.einsum('bshd,bmhd->bhsm', q, kv[:, :, 0]).astype(jnp.float32) * (XA_HEAD_DIM ** -0.5)
    a = jax.nn.softmax(s, axis=-1)
    o = jnp.einsum('bhsm,bmhd->bshd', a.astype(x.dtype), kv[:, :, 1]).reshape(B_, S_, D_MODEL)
    return o @ w_o


def peer(x, w_q, sub_keys, expert_u, expert_v):
    B_, S_, D = x.shape
    t = x.reshape(-1, D)
    T = t.shape[0]
    q = (t @ w_q).reshape(T, PEER_HEADS, 2, PEER_HALF)
    s = jnp.einsum('thcd,hcnd->thcn', q, sub_keys).astype(jnp.float32)
    sv, si = lax.top_k(s, PEER_TOPK)
    cand = (sv[:, :, 0, :, None] + sv[:, :, 1, None, :]).reshape(T, PEER_HEADS, PEER_TOPK * PEER_TOPK)
    cidx = (si[:, :, 0, :, None] * PEER_N_KEYS + si[:, :, 1, None, :]).reshape(T, PEER_HEADS, PEER_TOPK * PEER_TOPK)
    top_s, top_pos = lax.top_k(cand, PEER_TOPK)
    eidx = jnp.take_along_axis(cidx, top_pos, axis=-1)
    g = jax.nn.softmax(top_s, axis=-1).astype(x.dtype)
    nc = T // PEER_TOKEN_BLOCK

    def block(args):
        tb, ib, gb = args
        act = jax.nn.gelu(jnp.einsum('thkd,td->thk', expert_u[ib], tb), approximate=False)
        return jnp.einsum('thk,thkd->td', gb * act, expert_v[ib])

    y = lax.map(block, (t.reshape(nc, PEER_TOKEN_BLOCK, D),
                        eidx.reshape(nc, PEER_TOKEN_BLOCK, PEER_HEADS, PEER_TOPK),
                        g.reshape(nc, PEER_TOKEN_BLOCK, PEER_HEADS, PEER_TOPK)))
    return y.reshape(B_, S_, D)


def setup_inputs(seed: int = 0) -> dict:
    key = jax.random.key(seed)
    ks = iter(jax.random.split(key, 32))

    def nrm(shape, scale):
        return jax.random.normal(next(ks), shape, jnp.float32) * scale

    def gain(shape):
        return 1.0 + nrm(shape, 0.02)

    L = DEPTH
    return {
        'x': nrm((BATCH, SEQ, D_MODEL), 1.0),
        'mem': nrm((BATCH, N_MEM, D_MODEL), 1.0),
        'w_in': nrm((L, D_MODEL, IN_WIDTH), D_MODEL ** -0.5),
        'pool_w': nrm((L, POOL_GROUPS, POOL_GROUP_DIM, POOL_GROUP_DIM), POOL_GROUP_DIM ** -0.5),
        'pool_scale': gain((L, POOL_WIDTH)),
        'w_br_pool': nrm((L, POOL_WIDTH, D_MODEL), POOL_WIDTH ** -0.5),
        'lambda_q1': nrm((L, DA_QK_DIM), 0.1),
        'lambda_k1': nrm((L, DA_QK_DIM), 0.1),
        'lambda_q2': nrm((L, DA_QK_DIM), 0.1),
        'lambda_k2': nrm((L, DA_QK_DIM), 0.1),
        'subln_w': gain((L, DA_V_DIM)),
        'w_br_attn': nrm((L, DA_WIDTH, D_MODEL), DA_WIDTH ** -0.5),
        'w_out': nrm((L, D_MODEL, D_MODEL), DN_BETA * D_MODEL ** -0.5),
        'ln1_g': gain((L, D_MODEL)),
        'ln1_b': nrm((L, D_MODEL), 0.02),
        'w_cq': nrm((L, D_MODEL, D_MODEL), D_MODEL ** -0.5),
        'w_ckv': nrm((L, D_MODEL, 2 * D_MODEL), D_MODEL ** -0.5),
        'w_co': nrm((L, D_MODEL, D_MODEL), DN_BETA * D_MODEL ** -0.5),
        'ln2_g': gain((L, D_MODEL)),
        'ln2_b': nrm((L, D_MODEL), 0.02),
        'w_pq': nrm((L, D_MODEL, PEER_HEADS * PEER_QUERY_DIM), D_MODEL ** -0.5),
        'sub_keys': nrm((L, PEER_HEADS, 2, PEER_N_KEYS, PEER_HALF), PEER_HALF ** -0.5),
        'expert_u': nrm((L, PEER_N_EXPERTS, D_MODEL), D_MODEL ** -0.5),
        'expert_v': nrm((L, PEER_N_EXPERTS, D_MODEL), DN_BETA * PEER_HEADS ** -0.5),
        'ln3_g': gain((L, D_MODEL)),
        'ln3_b': nrm((L, D_MODEL), 0.02),
    }


def reference(x, mem, w_in, pool_w, pool_scale, w_br_pool, lambda_q1, lambda_k1, lambda_q2, lambda_k2,
              subln_w, w_br_attn, w_out, ln1_g, ln1_b, w_cq, w_ckv, w_co, ln2_g, ln2_b,
              w_pq, sub_keys, expert_u, expert_v, ln3_g, ln3_b):
    B_, S_, _ = x.shape
    for l in range(DEPTH):
        lam_init = 0.8 - 0.6 * math.exp(-0.3 * l)
        h = x @ w_in[l]
        p, q, k, v, gate_p, gate_a = jnp.split(h, IN_SPLITS, axis=-1)
        pooled = causal_multiscale_pool(p)
        y_pool = jnp.einsum('bsgc,gcd->bsgd', pooled, pool_w[l]).reshape(B_, S_, POOL_WIDTH) * pool_scale[l]
        y_pool = y_pool @ w_br_pool[l]
        lam = (jnp.exp(jnp.sum(lambda_q1[l].astype(jnp.float32) * lambda_k1[l].astype(jnp.float32)))
               - jnp.exp(jnp.sum(lambda_q2[l].astype(jnp.float32) * lambda_k2[l].astype(jnp.float32)))
               + lam_init)
        y_attn = diff_attention(q, k, v, lam, lam_init, subln_w[l]) @ w_br_attn[l]
        merged = jax.nn.sigmoid(gate_p) * y_pool + jax.nn.sigmoid(gate_a) * y_attn
        x = layer_norm(DN_ALPHA * x + merged @ w_out[l], ln1_g[l], ln1_b[l])
        x = layer_norm(DN_ALPHA * x + memory_cross_attention(x, mem, w_cq[l], w_ckv[l], w_co[l]), ln2_g[l], ln2_b[l])
        x = layer_norm(DN_ALPHA * x + peer(x, w_pq[l], sub_keys[l], expert_u[l], expert_v[l]), ln3_g[l], ln3_b[l])
    return x
```

```python
import functools
import math

import jax
import jax.numpy as jnp
from jax import lax
from jax.experimental import pallas as pl
from jax.experimental.pallas import tpu as pltpu

F32 = jnp.float32
BF16 = jnp.bfloat16

D_MODEL = 1024
POOL_WINDOWS = (2, 4, 8, 16)
POOL_GROUPS = len(POOL_WINDOWS)
POOL_WIDTH = D_MODEL // 2
POOL_GROUP_DIM = POOL_WIDTH // POOL_GROUPS
POOL_HALO = max(POOL_WINDOWS)
DA_HEADS = 8
DA_QK_DIM = 64
DA_V_DIM = 2 * DA_QK_DIM
DA_WIDTH = DA_HEADS * DA_V_DIM
XA_HEADS = 4
XA_HEAD_DIM = D_MODEL // XA_HEADS
PEER_HEADS = 8
PEER_N_KEYS = 128
PEER_N_EXPERTS = PEER_N_KEYS * PEER_N_KEYS
PEER_QUERY_DIM = 128
PEER_HALF = PEER_QUERY_DIM // 2
PEER_TOPK = 16
DEPTH = 1
DN_ALPHA = (2 * DEPTH) ** 0.25
LN_EPS = 1e-5
NEG_INF = -1e30
SQRT_HALF = math.sqrt(0.5)

VMEM_LIMIT_BYTES = 52 * 1024 * 1024

_NT = (((1,), (1,)), ((), ()))


def _params(*sem):
    return pltpu.CompilerParams(dimension_semantics=sem, vmem_limit_bytes=VMEM_LIMIT_BYTES)


def _dot(a, b):
    return jnp.dot(a, b, preferred_element_type=F32)


def _dot_nt(a, b):
    return lax.dot_general(a, b, _NT, preferred_element_type=F32)


def _layer_norm(z, g, b):
    mu = jnp.mean(z, axis=-1, keepdims=True)
    zc = z - mu
    var = jnp.mean(zc * zc, axis=-1, keepdims=True)
    return zc * lax.rsqrt(var + LN_EPS) * g + b


def _proj_kernel(a_ref, w_ref, o_ref):
    o_ref[...] = _dot(a_ref[...].astype(BF16), w_ref[...]).astype(o_ref.dtype)


def _proj(a, w, col_block0, n_cols, out_dtype, tm, tn, name):
    m, k = a.shape
    return pl.pallas_call(
        _proj_kernel,
        grid=(m // tm, n_cols // tn),
        in_specs=[
            pl.BlockSpec((tm, k), lambda i, j: (i, 0)),
            pl.BlockSpec((k, tn), lambda i, j: (0, j + col_block0)),
        ],
        out_specs=pl.BlockSpec((tm, tn), lambda i, j: (i, j)),
        out_shape=jax.ShapeDtypeStruct((m, n_cols), out_dtype),
        compiler_params=_params("parallel", "parallel"),
        name=name,
    )(a, w)


def _proj_t_kernel(wt_ref, a_ref, o_ref):
    o_ref[0] = _dot_nt(wt_ref[...], a_ref[...].astype(BF16)).astype(o_ref.dtype)


def _proj_t_blocked(a, wt, tk, name):
    m, k = a.shape
    n = wt.shape[0]
    return pl.pallas_call(
        _proj_t_kernel,
        grid=(m // tk,),
        in_specs=[
            pl.BlockSpec((n, k), lambda j: (0, 0)),
            pl.BlockSpec((tk, k), lambda j: (j, 0)),
        ],
        out_specs=pl.BlockSpec((1, n, tk), lambda j: (j, 0, 0)),
        out_shape=jax.ShapeDtypeStruct((m // tk, n, tk), BF16),
        compiler_params=_params("parallel"),
        name=name,
    )(wt, a)


def _attn_kernel(slopes_ref, lq1_ref, lk1_ref, lq2_ref, lk2_ref, sw_ref, q_ref, k_ref, vt_ref, o_ref,
                 bias_ref, m1_ref, l1_ref, a1_ref, m2_ref, l2_ref, a2_ref, *, t, lam_init):
    h = pl.program_id(1)
    i = pl.program_id(2)
    slope = slopes_ref[h]

    q = (q_ref[...].astype(F32) * (DA_QK_DIM ** -0.5)).astype(BF16)
    lane = lax.broadcasted_iota(jnp.int32, q.shape, 1)
    zero = jnp.zeros_like(q)
    q_maps = (jnp.where(lane < DA_QK_DIM, q, zero), jnp.where(lane >= DA_QK_DIM, q, zero))

    rel = (lax.broadcasted_iota(jnp.int32, (t, t), 1) - lax.broadcasted_iota(jnp.int32, (t, t), 0)).astype(F32)
    bias_ref[...] = -slope * rel

    for m_ref, l_ref, a_ref in ((m1_ref, l1_ref, a1_ref), (m2_ref, l2_ref, a2_ref)):
        m_ref[...] = jnp.full(m_ref.shape, NEG_INF, F32)
        l_ref[...] = jnp.zeros(l_ref.shape, F32)
        a_ref[...] = jnp.zeros(a_ref.shape, F32)

    def step(j, bias):
        k_j = k_ref[pl.ds(pl.multiple_of(j * t, t), t), :]
        vt_j = vt_ref[j]
        for qm, m_ref, l_ref, a_ref in ((q_maps[0], m1_ref, l1_ref, a1_ref), (q_maps[1], m2_ref, l2_ref, a2_ref)):
            s = _dot_nt(k_j, qm) + bias
            m_old = m_ref[...]
            m_new = jnp.maximum(m_old, jnp.max(s, axis=0, keepdims=True))
            alpha = jnp.exp(m_old - m_new)
            p = jnp.exp(s - m_new)
            l_ref[...] = alpha * l_ref[...] + jnp.sum(p, axis=0, keepdims=True)
            a_ref[...] = alpha * a_ref[...] + _dot(vt_j, p.astype(BF16))
            m_ref[...] = m_new

    def off_diagonal(j, carry):
        tile_shift = -slope * ((i - j) * t).astype(F32)
        step(j, bias_ref[...] + tile_shift)
        return carry

    lax.fori_loop(0, i, off_diagonal, 0)
    step(i, jnp.where(rel >= 0, bias_ref[...], NEG_INF))

    lam = (jnp.exp(jnp.sum(lq1_ref[...] * lk1_ref[...], axis=-1, keepdims=True))
           - jnp.exp(jnp.sum(lq2_ref[...] * lk2_ref[...], axis=-1, keepdims=True)) + lam_init)
    o_t = a1_ref[...] / l1_ref[...] - lam * (a2_ref[...] / l2_ref[...])
    o = o_t.T
    o = o * lax.rsqrt(jnp.mean(o * o, axis=-1, keepdims=True) + LN_EPS)
    o_ref[...] = (o * sw_ref[...] * (1.0 - lam_init)).astype(o_ref.dtype)


def _diff_attention(qk, vtb, slopes, lq1, lk1, lq2, lk2, subln_w, batch, seq, lam_init, t):
    n_tok = qk.shape[0]
    nq = seq // t
    vec = pl.BlockSpec((1, DA_QK_DIM), lambda b, h, i: (0, 0))
    scratch = [pltpu.VMEM((t, t), F32)]
    for _ in range(2):
        scratch += [pltpu.VMEM((1, t), F32), pltpu.VMEM((1, t), F32), pltpu.VMEM((DA_V_DIM, t), F32)]
    return pl.pallas_call(
        functools.partial(_attn_kernel, t=t, lam_init=lam_init),
        grid=(batch, DA_HEADS, nq),
        in_specs=[
            pl.BlockSpec(memory_space=pltpu.SMEM),
            vec, vec, vec, vec,
            pl.BlockSpec((1, DA_V_DIM), lambda b, h, i: (0, 0)),
            pl.BlockSpec((t, DA_V_DIM), lambda b, h, i: (b * nq + i, h)),
            pl.BlockSpec((seq, DA_V_DIM), lambda b, h, i: (b, DA_HEADS + h)),
            pl.BlockSpec((nq, DA_V_DIM, t), lambda b, h, i: (b, h, 0)),
        ],
        out_specs=pl.BlockSpec((t, DA_V_DIM), lambda b, h, i: (b * nq + i, h)),
        out_shape=jax.ShapeDtypeStruct((n_tok, DA_WIDTH), BF16),
        scratch_shapes=scratch,
        compiler_params=_params("parallel", "parallel", "arbitrary"),
        name="diff_attention",
    )(slopes, lq1, lk1, lq2, lk2, subln_w, qk, qk, vtb)


def _mix_kernel(x_ref, xh_ref, o_ref, wp_ref, wgp_ref, wga_ref, poolw_ref, pscale_ref, wbp_ref, wba_ref,
                wout_ref, g_ref, b_ref, out_ref, e_ref, *, tm, seq):
    i = pl.program_id(0)
    start = (i * tm) % seq
    x = x_ref[...]
    xb = x.astype(BF16)

    p = _dot(xb, wp_ref[...])
    p_halo = _dot(xh_ref[...].astype(BF16), wp_ref[...])
    e_ref[0:POOL_HALO, :] = jnp.where(start == 0, 0.0, p_halo)
    e_ref[POOL_HALO:, :] = p

    pos = (start + 1 + lax.broadcasted_iota(jnp.int32, (tm, 1), 0)).astype(F32)
    ys = []
    for g, w in enumerate(POOL_WINDOWS):
        cols = slice(g * POOL_GROUP_DIM, (g + 1) * POOL_GROUP_DIM)
        acc = e_ref[POOL_HALO:POOL_HALO + tm, cols]
        for back in range(1, w):
            acc = acc + e_ref[POOL_HALO - back:POOL_HALO - back + tm, cols]
        pooled = acc / jnp.minimum(pos, float(w)) - e_ref[POOL_HALO:POOL_HALO + tm, cols]
        ys.append(_dot(pooled.astype(BF16), poolw_ref[g]))
    y = jnp.concatenate(ys, axis=-1) * pscale_ref[...]
    y_pool = _dot(y.astype(BF16), wbp_ref[...])

    gate_p = _dot(xb, wgp_ref[...])
    gate_a = _dot(xb, wga_ref[...])
    y_attn = _dot(o_ref[...], wba_ref[...])
    merged = jax.nn.sigmoid(gate_p) * y_pool + jax.nn.sigmoid(gate_a) * y_attn
    z = DN_ALPHA * x + _dot(merged.astype(BF16), wout_ref[...])
    out_ref[...] = _layer_norm(z, g_ref[...], b_ref[...])


def _mix_ln1(x2d, o, w_p, w_gp, w_ga, pool_w, pool_scale, w_bp, w_ba, w_out, g, b, seq, tm):
    n_tok = x2d.shape[0]
    halo_blocks = tm // POOL_HALO

    def full(a):
        return pl.BlockSpec(a.shape, lambda i: (0,) * a.ndim)

    return pl.pallas_call(
        functools.partial(_mix_kernel, tm=tm, seq=seq),
        grid=(n_tok // tm,),
        in_specs=[
            pl.BlockSpec((tm, D_MODEL), lambda i: (i, 0)),
            pl.BlockSpec((POOL_HALO, D_MODEL), lambda i: (jnp.maximum(i * halo_blocks - 1, 0), 0)),
            pl.BlockSpec((tm, DA_WIDTH), lambda i: (i, 0)),
            full(w_p), full(w_gp), full(w_ga), full(pool_w), full(pool_scale), full(w_bp), full(w_ba),
            full(w_out), full(g), full(b),
        ],
        out_specs=pl.BlockSpec((tm, D_MODEL), lambda i: (i, 0)),
        out_shape=jax.ShapeDtypeStruct((n_tok, D_MODEL), F32),
        scratch_shapes=[pltpu.VMEM((tm + POOL_HALO, POOL_WIDTH), F32)],
        compiler_params=_params("parallel"),
        name="mix_ln1",
    )(x2d, x2d, o, w_p, w_gp, w_ga, pool_w, pool_scale, w_bp, w_ba, w_out, g, b)


def _xattn_kernel(x_ref, kv_ref, wq_ref, wo_ref, g_ref, b_ref, out_ref):
    x = x_ref[...]
    q = (_dot(x.astype(BF16), wq_ref[...]) * (XA_HEAD_DIM ** -0.5)).astype(BF16)
    kv = kv_ref[0]
    outs = []
    for h in range(XA_HEADS):
        cols = slice(h * XA_HEAD_DIM, (h + 1) * XA_HEAD_DIM)
        s = _dot_nt(q[:, cols], kv[:, cols])
        p = jnp.exp(s - jnp.max(s, axis=-1, keepdims=True))
        l = jnp.sum(p, axis=-1, keepdims=True)
        v = kv[:, D_MODEL + h * XA_HEAD_DIM:D_MODEL + (h + 1) * XA_HEAD_DIM]
        outs.append(_dot(p.astype(BF16), v) / l)
    o = jnp.concatenate(outs, axis=-1)
    z = DN_ALPHA * x + _dot(o.astype(BF16), wo_ref[...])
    out_ref[...] = _layer_norm(z, g_ref[...], b_ref[...])


def _xattn_ln2(x1, kvm, w_cq, w_co, g, b, seq, tm):
    n_tok = x1.shape[0]
    n_mem = kvm.shape[1]

    def full(a):
        return pl.BlockSpec(a.shape, lambda i: (0,) * a.ndim)

    return pl.pallas_call(
        _xattn_kernel,
        grid=(n_tok // tm,),
        in_specs=[
            pl.BlockSpec((tm, D_MODEL), lambda i: (i, 0)),
            pl.BlockSpec((1, n_mem, 2 * D_MODEL), lambda i: ((i * tm) // seq, 0, 0)),
            full(w_cq), full(w_co), full(g), full(b),
        ],
        out_specs=pl.BlockSpec((tm, D_MODEL), lambda i: (i, 0)),
        out_shape=jax.ShapeDtypeStruct((n_tok, D_MODEL), F32),
        compiler_params=_params("parallel"),
        name="xattn_ln2",
    )(x1, kvm, w_cq, w_co, g, b)


def _top_values(s, n, with_rank):
    vals = []
    rank = jnp.full(s.shape, float(n), F32)
    cur = s
    for a in range(n):
        m = jnp.max(cur, axis=0, keepdims=True)
        vals.append(m)
        hit = cur == m
        if with_rank:
            rank = jnp.where(hit, float(a), rank)
        cur = jnp.where(hit, -jnp.inf, cur)
    return vals, rank


def _route_kernel(x_ref, wq_ref, keys_ref, rank2_ref, e2_ref, count_ref, d_ref, v1_ref, v2_ref):
    k = PEER_TOPK
    q_t = _dot_nt(wq_ref[...], x_ref[...].astype(BF16)).astype(BF16)
    s1 = _dot(keys_ref[0, 0], q_t[:PEER_HALF])
    s2 = _dot(keys_ref[0, 1], q_t[PEER_HALF:])

    v1, _ = _top_values(s1, k, False)
    v2, rank2 = _top_values(s2, k, True)
    for a in range(k):
        v1_ref[a:a + 1, :] = v1[a]
        v2_ref[a:a + 1, :] = v2[a]
    v1_all = v1_ref[...]

    cand = jnp.concatenate([v1_all + v2[b] for b in range(k // 2)] + [v2_ref[k // 2:, :] + v1[0]], axis=0)
    m0 = v1[0] + v2[0]
    tau = jnp.full(m0.shape, jnp.inf, F32)
    seen = jnp.zeros(m0.shape, F32)
    z = jnp.zeros(m0.shape, F32)
    cur = cand
    for _ in range(k):
        m = jnp.max(cur, axis=0, keepdims=True)
        hit = cur == m
        cnt = jnp.sum(hit.astype(F32), axis=0, keepdims=True)
        open_ = seen < float(k)
        z = z + jnp.where(open_, cnt * jnp.exp(m - m0), 0.0)
        seen = seen + cnt
        tau = jnp.where(open_, m, tau)
        cur = jnp.where(hit, -jnp.inf, cur)

    count = jnp.zeros(s1.shape, F32)
    for b in range(k):
        count = count + jnp.where(s1 + v2[b] >= tau, 1.0, 0.0)

    rank2_ref[0] = rank2
    e2_ref[0] = jnp.exp(s2 - v2[0])
    count_ref[0] = count
    d_ref[0] = jnp.exp(s1 - v1[0]) / z


def _peer_route(x2, wq_t, keys, tm):
    n_tok = x2.shape[0]
    out = jax.ShapeDtypeStruct((PEER_HEADS, PEER_N_KEYS, n_tok), F32)
    out_spec = pl.BlockSpec((1, PEER_N_KEYS, tm), lambda i, h: (h, 0, i))
    return pl.pallas_call(
        _route_kernel,
        grid=(n_tok // tm, PEER_HEADS),
        in_specs=[
            pl.BlockSpec((tm, D_MODEL), lambda i, h: (i, 0)),
            pl.BlockSpec((PEER_QUERY_DIM, D_MODEL), lambda i, h: (h, 0)),
            pl.BlockSpec((1, 2, PEER_N_KEYS, PEER_HALF), lambda i, h: (h, 0, 0, 0)),
        ],
        out_specs=[out_spec] * 4,
        out_shape=[out] * 4,
        scratch_shapes=[pltpu.VMEM((PEER_TOPK, tm), F32), pltpu.VMEM((PEER_TOPK, tm), F32)],
        compiler_params=_params("parallel", "parallel"),
        name="peer_route",
    )(x2, wq_t, keys)


def _peer_kernel(x_ref, u_ref, vt_ref, rank2_ref, e2_ref, count_ref, d_ref, g_ref, b_ref, out_ref,
                 xb_ref, acc_ref, w_ref, *, rows_per_tile):
    e = pl.program_id(1)

    @pl.when(e == 0)
    def _():
        xb_ref[...] = x_ref[...].astype(BF16)
        acc_ref[...] = jnp.zeros(acc_ref.shape, F32)

    a_t = _dot_nt(u_ref[...], xb_ref[...])
    for r in range(rows_per_tile):
        i = e * rows_per_tile + r
        a = a_t[r * PEER_N_KEYS:(r + 1) * PEER_N_KEYS]
        act = 0.5 * a * (1.0 + lax.erf(a * SQRT_HALF))
        gate = jnp.zeros(a.shape, F32)
        for h in range(PEER_HEADS):
            cnt = count_ref[h, pl.ds(i, 1), :]
            d = d_ref[h, pl.ds(i, 1), :]
            gate = gate + jnp.where(rank2_ref[h] < cnt, e2_ref[h], 0.0) * d
        w_ref[r * PEER_N_KEYS:(r + 1) * PEER_N_KEYS, :] = (gate * act).astype(BF16)
    acc_ref[...] += _dot(vt_ref[...], w_ref[...])

    @pl.when(e == pl.num_programs(1) - 1)
    def _():
        z = DN_ALPHA * x_ref[...] + acc_ref[...].T
        out_ref[...] = _layer_norm(z, g_ref[...], b_ref[...])


def _peer_ln3(x2, u, vt, rank2, e2, count, d, g, b, tb, te):
    n_tok = x2.shape[0]
    rows_per_tile = te // PEER_N_KEYS
    route_spec = pl.BlockSpec((PEER_HEADS, PEER_N_KEYS, tb), lambda t, e: (0, 0, t))
    vec = pl.BlockSpec((1, D_MODEL), lambda t, e: (0, 0))
    return pl.pallas_call(
        functools.partial(_peer_kernel, rows_per_tile=rows_per_tile),
        grid=(n_tok // tb, PEER_N_EXPERTS // te),
        in_specs=[
            pl.BlockSpec((tb, D_MODEL), lambda t, e: (t, 0)),
            pl.BlockSpec((te, D_MODEL), lambda t, e: (e, 0)),
            pl.BlockSpec((D_MODEL, te), lambda t, e: (0, e)),
            route_spec, route_spec, route_spec, route_spec,
            vec, vec,
        ],
        out_specs=pl.BlockSpec((tb, D_MODEL), lambda t, e: (t, 0)),
        out_shape=jax.ShapeDtypeStruct((n_tok, D_MODEL), F32),
        scratch_shapes=[
            pltpu.VMEM((tb, D_MODEL), BF16),
            pltpu.VMEM((D_MODEL, tb), F32),
            pltpu.VMEM((te, tb), BF16),
        ],
        compiler_params=_params("parallel", "arbitrary"),
        name="peer_dense",
    )(x2, u, vt, rank2, e2, count, d, g, b)


def kernel(x, mem, w_in, pool_w, pool_scale, w_br_pool, lambda_q1, lambda_k1, lambda_q2, lambda_k2, subln_w, w_br_attn, w_out, ln1_g, ln1_b, w_cq, w_ckv, w_co, ln2_g, ln2_b, w_pq, sub_keys, expert_u, expert_v, ln3_g, ln3_b):
    batch, seq, _ = x.shape
    n_tok = batch * seq
    n_mem = mem.shape[1]
    attn_tile = 512
    slopes = jnp.exp2(-8.0 * jnp.arange(1, DA_HEADS + 1, dtype=F32) / DA_HEADS)

    def row(v):
        return v.reshape(1, -1).astype(F32)

    for l in range(DEPTH):
        lam_init = 0.8 - 0.6 * math.exp(-0.3 * l)
        x2d = x.reshape(n_tok, D_MODEL)
        w_in_b = w_in[l].astype(BF16)
        q0 = POOL_WIDTH
        k0 = q0 + DA_WIDTH
        v0 = k0 + DA_WIDTH
        gp0 = v0 + DA_WIDTH
        ga0 = gp0 + D_MODEL

        qk = _proj(x2d, w_in_b, q0 // 512, 2 * DA_WIDTH, BF16, 1024, 512, "proj_qk")
        vtb = _proj_t_blocked(x2d, w_in_b[:, v0:gp0].T, attn_tile, "proj_vt")
        o = _diff_attention(qk, vtb, slopes, row(lambda_q1[l]), row(lambda_k1[l]), row(lambda_q2[l]),
                            row(lambda_k2[l]), row(subln_w[l]), batch, seq, lam_init, attn_tile)

        x1 = _mix_ln1(x2d, o, w_in_b[:, :POOL_WIDTH], w_in_b[:, gp0:ga0], w_in_b[:, ga0:],
                      pool_w[l].astype(BF16), row(pool_scale[l]), w_br_pool[l].astype(BF16),
                      w_br_attn[l].astype(BF16), w_out[l].astype(BF16), row(ln1_g[l]), row(ln1_b[l]), seq, 256)

        kvm = _proj(mem.reshape(batch * n_mem, D_MODEL), w_ckv[l].astype(BF16), 0, 2 * D_MODEL, BF16,
                    batch * n_mem, 512, "proj_mem_kv").reshape(batch, n_mem, 2 * D_MODEL)
        x2 = _xattn_ln2(x1, kvm, w_cq[l].astype(BF16), w_co[l].astype(BF16), row(ln2_g[l]), row(ln2_b[l]), seq, 256)

        rank2, e2, count, d = _peer_route(x2, w_pq[l].T.astype(BF16), sub_keys[l].astype(BF16), 256)
        x3 = _peer_ln3(x2, expert_u[l].astype(BF16), expert_v[l].T.astype(BF16), rank2, e2, count, d,
                       row(ln3_g[l]), row(ln3_b[l]), 512, 512)
        x = x3.reshape(batch, seq, D_MODEL)
    return x
```

```python
import functools
import math

import jax
import jax.numpy as jnp
from jax import lax
from jax.experimental import pallas as pl
from jax.experimental.pallas import tpu as pltpu

F32 = jnp.float32
BF16 = jnp.bfloat16

D_MODEL = 1024
POOL_WINDOWS = (2, 4, 8, 16)
POOL_GROUPS = len(POOL_WINDOWS)
POOL_WIDTH = D_MODEL // 2
POOL_GROUP_DIM = POOL_WIDTH // POOL_GROUPS
POOL_HALO = max(POOL_WINDOWS)
DA_HEADS = 8
DA_QK_DIM = 64
DA_V_DIM = 2 * DA_QK_DIM
DA_WIDTH = DA_HEADS * DA_V_DIM
XA_HEADS = 4
XA_HEAD_DIM = D_MODEL // XA_HEADS
PEER_HEADS = 8
PEER_N_KEYS = 128
PEER_N_EXPERTS = PEER_N_KEYS * PEER_N_KEYS
PEER_QUERY_DIM = 128
PEER_HALF = PEER_QUERY_DIM // 2
PEER_TOPK = 16
DEPTH = 1
DN_ALPHA = (2 * DEPTH) ** 0.25
LN_EPS = 1e-5
NEG_INF = -1e30
SQRT_HALF = math.sqrt(0.5)

VMEM_LIMIT_BYTES = 52 * 1024 * 1024

_NT = (((1,), (1,)), ((), ()))


def _params(*sem):
    return pltpu.CompilerParams(dimension_semantics=sem, vmem_limit_bytes=VMEM_LIMIT_BYTES)


def _dot(a, b):
    return jnp.dot(a, b, preferred_element_type=F32)


def _dot_nt(a, b):
    return lax.dot_general(a, b, _NT, preferred_element_type=F32)


def _layer_norm(z, g, b):
    mu = jnp.mean(z, axis=-1, keepdims=True)
    zc = z - mu
    var = jnp.mean(zc * zc, axis=-1, keepdims=True)
    return zc * lax.rsqrt(var + LN_EPS) * g + b


def _proj_kernel(a_ref, w_ref, o_ref):
    o_ref[...] = _dot(a_ref[...].astype(BF16), w_ref[...]).astype(o_ref.dtype)


def _proj(a, w, col_block0, n_cols, out_dtype, tm, tn, name):
    m, k = a.shape
    return pl.pallas_call(
        _proj_kernel,
        grid=(m // tm, n_cols // tn),
        in_specs=[
            pl.BlockSpec((tm, k), lambda i, j: (i, 0)),
            pl.BlockSpec((k, tn), lambda i, j: (0, j + col_block0)),
        ],
        out_specs=pl.BlockSpec((tm, tn), lambda i, j: (i, j)),
        out_shape=jax.ShapeDtypeStruct((m, n_cols), out_dtype),
        compiler_params=_params("parallel", "parallel"),
        name=name,
    )(a, w)


def _proj_t_kernel(wt_ref, a_ref, o_ref):
    o_ref[0] = _dot_nt(wt_ref[...], a_ref[...].astype(BF16)).astype(o_ref.dtype)


def _proj_t_blocked(a, wt, tk, name):
    m, k = a.shape
    n = wt.shape[0]
    return pl.pallas_call(
        _proj_t_kernel,
        grid=(m // tk,),
        in_specs=[
            pl.BlockSpec((n, k), lambda j: (0, 0)),
            pl.BlockSpec((tk, k), lambda j: (j, 0)),
        ],
        out_specs=pl.BlockSpec((1, n, tk), lambda j: (j, 0, 0)),
        out_shape=jax.ShapeDtypeStruct((m // tk, n, tk), BF16),
        compiler_params=_params("parallel"),
        name=name,
    )(wt, a)


def _attn_kernel(slopes_ref, lq1_ref, lk1_ref, lq2_ref, lk2_ref, sw_ref, q_ref, k_ref, vt_ref, o_ref,
                 bias_ref, m1_ref, l1_ref, a1_ref, m2_ref, l2_ref, a2_ref, *, t, lam_init):
    h = pl.program_id(1)
    i = pl.program_id(2)
    slope = slopes_ref[h]

    q = (q_ref[...].astype(F32) * (DA_QK_DIM ** -0.5)).astype(BF16)
    lane = lax.broadcasted_iota(jnp.int32, q.shape, 1)
    zero = jnp.zeros_like(q)
    q_maps = (jnp.where(lane < DA_QK_DIM, q, zero), jnp.where(lane >= DA_QK_DIM, q, zero))

    rel = (lax.broadcasted_iota(jnp.int32, (t, t), 1) - lax.broadcasted_iota(jnp.int32, (t, t), 0)).astype(F32)
    bias_ref[...] = -slope * rel

    for m_ref, l_ref, a_ref in ((m1_ref, l1_ref, a1_ref), (m2_ref, l2_ref, a2_ref)):
        m_ref[...] = jnp.full(m_ref.shape, NEG_INF, F32)
        l_ref[...] = jnp.zeros(l_ref.shape, F32)
        a_ref[...] = jnp.zeros(a_ref.shape, F32)

    def step(j, bias):
        k_j = k_ref[pl.ds(pl.multiple_of(j * t, t), t), :]
        vt_j = vt_ref[j]
        for qm, m_ref, l_ref, a_ref in ((q_maps[0], m1_ref, l1_ref, a1_ref), (q_maps[1], m2_ref, l2_ref, a2_ref)):
            s = _dot_nt(k_j, qm) + bias
            m_old = m_ref[...]
            m_new = jnp.maximum(m_old, jnp.max(s, axis=0, keepdims=True))
            alpha = jnp.exp(m_old - m_new)
            p = jnp.exp(s - m_new)
            l_ref[...] = alpha * l_ref[...] + jnp.sum(p, axis=0, keepdims=True)
            a_ref[...] = alpha * a_ref[...] + _dot(vt_j, p.astype(BF16))
            m_ref[...] = m_new

    def off_diagonal(j, carry):
        tile_shift = -slope * ((i - j) * t).astype(F32)
        step(j, bias_ref[...] + tile_shift)
        return carry

    lax.fori_loop(0, i, off_diagonal, 0)
    step(i, jnp.where(rel >= 0, bias_ref[...], NEG_INF))

    lam = (jnp.exp(jnp.sum(lq1_ref[...] * lk1_ref[...], axis=-1, keepdims=True))
           - jnp.exp(jnp.sum(lq2_ref[...] * lk2_ref[...], axis=-1, keepdims=True)) + lam_init)
    o_t = a1_ref[...] / l1_ref[...] - lam * (a2_ref[...] / l2_ref[...])
    o = o_t.T
    o = o * lax.rsqrt(jnp.mean(o * o, axis=-1, keepdims=True) + LN_EPS)
    o_ref[...] = (o * sw_ref[...] * (1.0 - lam_init)).astype(o_ref.dtype)


def _diff_attention(qk, vtb, slopes, lq1, lk1, lq2, lk2, subln_w, batch, seq, lam_init, t):
    n_tok = qk.shape[0]
    nq = seq // t
    vec = pl.BlockSpec((1, DA_QK_DIM), lambda b, h, i: (0, 0))
    scratch = [pltpu.VMEM((t, t), F32)]
    for _ in range(2):
        scratch += [pltpu.VMEM((1, t), F32), pltpu.VMEM((1, t), F32), pltpu.VMEM((DA_V_DIM, t), F32)]
    return pl.pallas_call(
        functools.partial(_attn_kernel, t=t, lam_init=lam_init),
        grid=(batch, DA_HEADS, nq),
        in_specs=[
            pl.BlockSpec(memory_space=pltpu.SMEM),
            vec, vec, vec, vec,
            pl.BlockSpec((1, DA_V_DIM), lambda b, h, i: (0, 0)),
            pl.BlockSpec((t, DA_V_DIM), lambda b, h, i: (b * nq + i, h)),
            pl.BlockSpec((seq, DA_V_DIM), lambda b, h, i: (b, DA_HEADS + h)),
            pl.BlockSpec((nq, DA_V_DIM, t), lambda b, h, i: (b, h, 0)),
        ],
        out_specs=pl.BlockSpec((t, DA_V_DIM), lambda b, h, i: (b * nq + i, h)),
        out_shape=jax.ShapeDtypeStruct((n_tok, DA_WIDTH), BF16),
        scratch_shapes=scratch,
        compiler_params=_params("parallel", "parallel", "arbitrary"),
        name="diff_attention",
    )(slopes, lq1, lk1, lq2, lk2, subln_w, qk, qk, vtb)


def _mix_kernel(x_ref, xh_ref, o_ref, wp_ref, wgp_ref, wga_ref, poolw_ref, pscale_ref, wbp_ref, wba_ref,
                wout_ref, g_ref, b_ref, out_ref, e_ref, *, tm, seq):
    i = pl.program_id(0)
    start = (i * tm) % seq
    x = x_ref[...]
    xb = x.astype(BF16)

    p = _dot(xb, wp_ref[...])
    p_halo = _dot(xh_ref[...].astype(BF16), wp_ref[...])
    e_ref[0:POOL_HALO, :] = jnp.where(start == 0, 0.0, p_halo)
    e_ref[POOL_HALO:, :] = p

    pos = (start + 1 + lax.broadcasted_iota(jnp.int32, (tm, 1), 0)).astype(F32)
    ys = []
    for g, w in enumerate(POOL_WINDOWS):
        cols = slice(g * POOL_GROUP_DIM, (g + 1) * POOL_GROUP_DIM)
        acc = e_ref[POOL_HALO:POOL_HALO + tm, cols]
        for back in range(1, w):
            acc = acc + e_ref[POOL_HALO - back:POOL_HALO - back + tm, cols]
        pooled = acc / jnp.minimum(pos, float(w)) - e_ref[POOL_HALO:POOL_HALO + tm, cols]
        ys.append(_dot(pooled.astype(BF16), poolw_ref[g]))
    y = jnp.concatenate(ys, axis=-1) * pscale_ref[...]
    y_pool = _dot(y.astype(BF16), wbp_ref[...])

    gate_p = _dot(xb, wgp_ref[...])
    gate_a = _dot(xb, wga_ref[...])
    y_attn = _dot(o_ref[...], wba_ref[...])
    merged = jax.nn.sigmoid(gate_p) * y_pool + jax.nn.sigmoid(gate_a) * y_attn
    z = DN_ALPHA * x + _dot(merged.astype(BF16), wout_ref[...])
    out_ref[...] = _layer_norm(z, g_ref[...], b_ref[...])


def _mix_ln1(x2d, o, w_p, w_gp, w_ga, pool_w, pool_scale, w_bp, w_ba, w_out, g, b, seq, tm):
    n_tok = x2d.shape[0]
    halo_blocks = tm // POOL_HALO

    def full(a):
        return pl.BlockSpec(a.shape, lambda i: (0,) * a.ndim)

    return pl.pallas_call(
        functools.partial(_mix_kernel, tm=tm, seq=seq),
        grid=(n_tok // tm,),
        in_specs=[
            pl.BlockSpec((tm, D_MODEL), lambda i: (i, 0)),
            pl.BlockSpec((POOL_HALO, D_MODEL), lambda i: (jnp.maximum(i * halo_blocks - 1, 0), 0)),
            pl.BlockSpec((tm, DA_WIDTH), lambda i: (i, 0)),
            full(w_p), full(w_gp), full(w_ga), full(pool_w), full(pool_scale), full(w_bp), full(w_ba),
            full(w_out), full(g), full(b),
        ],
        out_specs=pl.BlockSpec((tm, D_MODEL), lambda i: (i, 0)),
        out_shape=jax.ShapeDtypeStruct((n_tok, D_MODEL), F32),
        scratch_shapes=[pltpu.VMEM((tm + POOL_HALO, POOL_WIDTH), F32)],
        compiler_params=_params("parallel"),
        name="mix_ln1",
    )(x2d, x2d, o, w_p, w_gp, w_ga, pool_w, pool_scale, w_bp, w_ba, w_out, g, b)


def _xattn_kernel(x_ref, kv_ref, wq_ref, wo_ref, g_ref, b_ref, out_ref):
    x = x_ref[...]
    q = (_dot(x.astype(BF16), wq_ref[...]) * (XA_HEAD_DIM ** -0.5)).astype(BF16)
    kv = kv_ref[0]
    outs = []
    for h in range(XA_HEADS):
        cols = slice(h * XA_HEAD_DIM, (h + 1) * XA_HEAD_DIM)
        s = _dot_nt(q[:, cols], kv[:, cols])
        p = jnp.exp(s - jnp.max(s, axis=-1, keepdims=True))
        l = jnp.sum(p, axis=-1, keepdims=True)
        v = kv[:, D_MODEL + h * XA_HEAD_DIM:D_MODEL + (h + 1) * XA_HEAD_DIM]
        outs.append(_dot(p.astype(BF16), v) / l)
    o = jnp.concatenate(outs, axis=-1)
    z = DN_ALPHA * x + _dot(o.astype(BF16), wo_ref[...])
    out_ref[...] = _layer_norm(z, g_ref[...], b_ref[...])


def _xattn_ln2(x1, kvm, w_cq, w_co, g, b, seq, tm):
    n_tok = x1.shape[0]
    n_mem = kvm.shape[1]

    def full(a):
        return pl.BlockSpec(a.shape, lambda i: (0,) * a.ndim)

    return pl.pallas_call(
        _xattn_kernel,
        grid=(n_tok // tm,),
        in_specs=[
            pl.BlockSpec((tm, D_MODEL), lambda i: (i, 0)),
            pl.BlockSpec((1, n_mem, 2 * D_MODEL), lambda i: ((i * tm) // seq, 0, 0)),
            full(w_cq), full(w_co), full(g), full(b),
        ],
        out_specs=pl.BlockSpec((tm, D_MODEL), lambda i: (i, 0)),
        out_shape=jax.ShapeDtypeStruct((n_tok, D_MODEL), F32),
        compiler_params=_params("parallel"),
        name="xattn_ln2",
    )(x1, kvm, w_cq, w_co, g, b)


def _top_values(s, n, with_rank):
    vals = []
    rank = jnp.full(s.shape, float(n), F32)
    cur = s
    for a in range(n):
        m = jnp.max(cur, axis=0, keepdims=True)
        vals.append(m)
        hit = cur == m
        if with_rank:
            rank = jnp.where(hit, float(a), rank)
        cur = jnp.where(hit, -jnp.inf, cur)
    return vals, rank


def _route_kernel(x_ref, wq_ref, keys_ref, rank2_ref, e2_ref, count_ref, d_ref, v1_ref, v2_ref):
    k = PEER_TOPK
    q_t = _dot_nt(wq_ref[...], x_ref[...].astype(BF16)).astype(BF16)
    s1 = _dot(keys_ref[0, 0], q_t[:PEER_HALF])
    s2 = _dot(keys_ref[0, 1], q_t[PEER_HALF:])

    v1, _ = _top_values(s1, k, False)
    v2, rank2 = _top_values(s2, k, True)
    for a in range(k):
        v1_ref[a:a + 1, :] = v1[a]
        v2_ref[a:a + 1, :] = v2[a]
    v1_all = v1_ref[...]

    cand = jnp.concatenate([v1_all + v2[0]] + [v1_all[:k // 2] + v2[b] for b in range(1, k // 2)]
                           + [v2_ref[k // 2:, :] + v1[0]], axis=0)
    m0 = v1[0] + v2[0]
    tau = jnp.full(m0.shape, jnp.inf, F32)
    seen = jnp.zeros(m0.shape, F32)
    z = jnp.zeros(m0.shape, F32)
    cur = cand
    for _ in range(k):
        m = jnp.max(cur, axis=0, keepdims=True)
        hit = cur == m
        cnt = jnp.sum(hit.astype(F32), axis=0, keepdims=True)
        open_ = seen < float(k)
        z = z + jnp.where(open_, cnt * jnp.exp(m - m0), 0.0)
        seen = seen + cnt
        tau = jnp.where(open_, m, tau)
        cur = jnp.where(hit, -jnp.inf, cur)

    count = jnp.zeros(s1.shape, F32)
    for b in range(k):
        count = count + jnp.where(s1 + v2[b] >= tau, 1.0, 0.0)

    rank2_ref[0] = rank2.astype(BF16)
    e2_ref[0] = jnp.exp(s2 - v2[0]).astype(BF16)
    count_ref[0] = count
    d_ref[0] = jnp.exp(s1 - v1[0]) / z


def _peer_route(x2, wq_t, keys, tm):
    n_tok = x2.shape[0]
    out_spec = pl.BlockSpec((1, PEER_N_KEYS, tm), lambda i, h: (h, 0, i))
    out_shape = [jax.ShapeDtypeStruct((PEER_HEADS, PEER_N_KEYS, n_tok), dt) for dt in (BF16, BF16, F32, F32)]
    return pl.pallas_call(
        _route_kernel,
        grid=(n_tok // tm, PEER_HEADS),
        in_specs=[
            pl.BlockSpec((tm, D_MODEL), lambda i, h: (i, 0)),
            pl.BlockSpec((PEER_QUERY_DIM, D_MODEL), lambda i, h: (h, 0)),
            pl.BlockSpec((1, 2, PEER_N_KEYS, PEER_HALF), lambda i, h: (h, 0, 0, 0)),
        ],
        out_specs=[out_spec] * 4,
        out_shape=out_shape,
        scratch_shapes=[pltpu.VMEM((PEER_TOPK, tm), F32), pltpu.VMEM((PEER_TOPK, tm), F32)],
        compiler_params=_params("parallel", "parallel"),
        name="peer_route",
    )(x2, wq_t, keys)


def _peer_kernel(x_ref, u_ref, vt_ref, rank2_ref, e2_ref, count_ref, d_ref, g_ref, b_ref, out_ref,
                 xb_ref, acc_ref, a_ref, *, chunks, rows_per_chunk):
    e = pl.program_id(1)

    @pl.when(e == 0)
    def _():
        xb_ref[...] = x_ref[...].astype(BF16)
        acc_ref[...] = jnp.zeros(acc_ref.shape, F32)

    xb = xb_ref[...]
    chunk = rows_per_chunk * PEER_N_KEYS
    tb = xb.shape[0]

    def sublane_rows(ref, h, row):
        tile = jnp.broadcast_to(ref[h, row:row + 1, :], (16, tb)).astype(BF16)
        return jnp.concatenate([tile] * (PEER_N_KEYS // 16), axis=0)

    def activations(c):
        return _dot_nt(u_ref[c * chunk:(c + 1) * chunk, :], xb)

    a_ref[0] = activations(0)
    for c in range(chunks):
        if c + 1 < chunks:
            a_ref[(c + 1) % 2] = activations(c + 1)
        parts = []
        for r in range(rows_per_chunk):
            row = c * rows_per_chunk + r
            a = a_ref[c % 2, r * PEER_N_KEYS:(r + 1) * PEER_N_KEYS, :]
            half = 0.5 * a
            act = half + half * lax.erf(a * SQRT_HALF)
            gate = None
            for h in range(PEER_HEADS):
                cnt = sublane_rows(count_ref, h, row)
                d = sublane_rows(d_ref, h, row)
                term = jnp.where(rank2_ref[h] < cnt, e2_ref[h], jnp.zeros((), BF16)) * d
                gate = term if gate is None else gate + term
            parts.append(gate * act.astype(BF16))
        w = jnp.concatenate(parts, axis=0)
        acc_ref[...] += _dot(vt_ref[:, c * chunk:(c + 1) * chunk], w)

    @pl.when(e == pl.num_programs(1) - 1)
    def _():
        z = DN_ALPHA * x_ref[...] + acc_ref[...].T
        out_ref[...] = _layer_norm(z, g_ref[...], b_ref[...])


def _peer_ln3(x2, u, vt, rank2, e2, count, d, g, b, tb, te, chunk):
    n_tok = x2.shape[0]
    rows_per_tile = te // PEER_N_KEYS
    key_spec = pl.BlockSpec((PEER_HEADS, PEER_N_KEYS, tb), lambda t, e: (0, 0, t))
    row_spec = pl.BlockSpec((PEER_HEADS, rows_per_tile, tb), lambda t, e: (0, e, t))
    vec = pl.BlockSpec((1, D_MODEL), lambda t, e: (0, 0))
    return pl.pallas_call(
        functools.partial(_peer_kernel, chunks=te // chunk, rows_per_chunk=chunk // PEER_N_KEYS),
        grid=(n_tok // tb, PEER_N_EXPERTS // te),
        in_specs=[
            pl.BlockSpec((tb, D_MODEL), lambda t, e: (t, 0)),
            pl.BlockSpec((te, D_MODEL), lambda t, e: (e, 0)),
            pl.BlockSpec((D_MODEL, te), lambda t, e: (0, e)),
            key_spec, key_spec, row_spec, row_spec,
            vec, vec,
        ],
        out_specs=pl.BlockSpec((tb, D_MODEL), lambda t, e: (t, 0)),
        out_shape=jax.ShapeDtypeStruct((n_tok, D_MODEL), F32),
        scratch_shapes=[
            pltpu.VMEM((tb, D_MODEL), BF16),
            pltpu.VMEM((D_MODEL, tb), F32),
            pltpu.VMEM((2, chunk, tb), F32),
        ],
        compiler_params=_params("parallel", "arbitrary"),
        name="peer_dense",
    )(x2, u, vt, rank2, e2, count, d, g, b)


def kernel(x, mem, w_in, pool_w, pool_scale, w_br_pool, lambda_q1, lambda_k1, lambda_q2, lambda_k2, subln_w, w_br_attn, w_out, ln1_g, ln1_b, w_cq, w_ckv, w_co, ln2_g, ln2_b, w_pq, sub_keys, expert_u, expert_v, ln3_g, ln3_b):
    batch, seq, _ = x.shape
    n_tok = batch * seq
    n_mem = mem.shape[1]
    attn_tile = 512
    slopes = jnp.exp2(-8.0 * jnp.arange(1, DA_HEADS + 1, dtype=F32) / DA_HEADS)

    def row(v):
        return v.reshape(1, -1).astype(F32)

    for l in range(DEPTH):
        lam_init = 0.8 - 0.6 * math.exp(-0.3 * l)
        x2d = x.reshape(n_tok, D_MODEL)
        w_in_b = w_in[l].astype(BF16)
        q0 = POOL_WIDTH
        k0 = q0 + DA_WIDTH
        v0 = k0 + DA_WIDTH
        gp0 = v0 + DA_WIDTH
        ga0 = gp0 + D_MODEL

        qk = _proj(x2d, w_in_b, q0 // 512, 2 * DA_WIDTH, BF16, 1024, 512, "proj_qk")
        vtb = _proj_t_blocked(x2d, w_in_b[:, v0:gp0].T, attn_tile, "proj_vt")
        o = _diff_attention(qk, vtb, slopes, row(lambda_q1[l]), row(lambda_k1[l]), row(lambda_q2[l]),
                            row(lambda_k2[l]), row(subln_w[l]), batch, seq, lam_init, attn_tile)

        x1 = _mix_ln1(x2d, o, w_in_b[:, :POOL_WIDTH], w_in_b[:, gp0:ga0], w_in_b[:, ga0:],
                      pool_w[l].astype(BF16), row(pool_scale[l]), w_br_pool[l].astype(BF16),
                      w_br_attn[l].astype(BF16), w_out[l].astype(BF16), row(ln1_g[l]), row(ln1_b[l]), seq, 256)

        kvm = _proj(mem.reshape(batch * n_mem, D_MODEL), w_ckv[l].astype(BF16), 0, 2 * D_MODEL, BF16,
                    batch * n_mem, 512, "proj_mem_kv").reshape(batch, n_mem, 2 * D_MODEL)
        x2 = _xattn_ln2(x1, kvm, w_cq[l].astype(BF16), w_co[l].astype(BF16), row(ln2_g[l]), row(ln2_b[l]), seq, 256)

        rank2, e2, count, d = _peer_route(x2, w_pq[l].T.astype(BF16), sub_keys[l].astype(BF16), 256)
        x3 = _peer_ln3(x2, expert_u[l].astype(BF16), expert_v[l].T.astype(BF16), rank2, e2, count, d,
                       row(ln3_g[l]), row(ln3_b[l]), 512, 1024, 256)
        x = x3.reshape(batch, seq, D_MODEL)
    return x
```

```python
import functools
import math

import jax
import jax.numpy as jnp
import numpy as np
from jax import lax
from jax.experimental import pallas as pl
from jax.experimental.pallas import tpu as pltpu

F32 = jnp.float32
BF16 = jnp.bfloat16

D_MODEL = 1024
POOL_WINDOWS = (2, 4, 8, 16)
POOL_GROUPS = len(POOL_WINDOWS)
POOL_WIDTH = D_MODEL // 2
POOL_GROUP_DIM = POOL_WIDTH // POOL_GROUPS
POOL_HALO = max(POOL_WINDOWS)
DA_HEADS = 8
DA_QK_DIM = 64
DA_V_DIM = 2 * DA_QK_DIM
DA_WIDTH = DA_HEADS * DA_V_DIM
XA_HEADS = 4
XA_HEAD_DIM = D_MODEL // XA_HEADS
PEER_HEADS = 8
PEER_N_KEYS = 128
PEER_N_EXPERTS = PEER_N_KEYS * PEER_N_KEYS
PEER_QUERY_DIM = 128
PEER_HALF = PEER_QUERY_DIM // 2
PEER_TOPK = 16
DEPTH = 1
DN_ALPHA = (2 * DEPTH) ** 0.25
LN_EPS = 1e-5
NEG_INF = -1e30
SQRT_HALF = math.sqrt(0.5)
LOG2E = float(np.float32(math.log2(math.e)))

VMEM_LIMIT_BYTES = 52 * 1024 * 1024

_NT = (((1,), (1,)), ((), ()))


def _params(*sem):
    return pltpu.CompilerParams(dimension_semantics=sem, vmem_limit_bytes=VMEM_LIMIT_BYTES)


def _dot(a, b):
    return jnp.dot(a, b, preferred_element_type=F32)


def _dot_nt(a, b):
    return lax.dot_general(a, b, _NT, preferred_element_type=F32)


def _layer_norm(z, g, b):
    mu = jnp.mean(z, axis=-1, keepdims=True)
    zc = z - mu
    var = jnp.mean(zc * zc, axis=-1, keepdims=True)
    return zc * lax.rsqrt(var + LN_EPS) * g + b


def _proj_kernel(a_ref, w_ref, scale_ref, o_ref):
    o_ref[...] = (_dot(a_ref[...].astype(BF16), w_ref[...]) * scale_ref[...]).astype(o_ref.dtype)


def _proj(a, w, col_block0, col_scale, out_dtype, tm, tn, name):
    m, k = a.shape
    n_cols = col_scale.shape[1]
    return pl.pallas_call(
        _proj_kernel,
        grid=(m // tm, n_cols // tn),
        in_specs=[
            pl.BlockSpec((tm, k), lambda i, j: (i, 0)),
            pl.BlockSpec((k, tn), lambda i, j: (0, j + col_block0)),
            pl.BlockSpec((1, tn), lambda i, j: (0, j)),
        ],
        out_specs=pl.BlockSpec((tm, tn), lambda i, j: (i, j)),
        out_shape=jax.ShapeDtypeStruct((m, n_cols), out_dtype),
        compiler_params=_params("parallel", "parallel"),
        name=name,
    )(a, w, col_scale)


def _proj_t_kernel(wt_ref, a_ref, o_ref):
    o_ref[0] = _dot_nt(wt_ref[...], a_ref[...].astype(BF16)).astype(o_ref.dtype)


def _proj_t_blocked(a, wt, tk, name):
    m, k = a.shape
    n = wt.shape[0]
    return pl.pallas_call(
        _proj_t_kernel,
        grid=(m // tk,),
        in_specs=[
            pl.BlockSpec((n, k), lambda j: (0, 0)),
            pl.BlockSpec((tk, k), lambda j: (j, 0)),
        ],
        out_specs=pl.BlockSpec((1, n, tk), lambda j: (j, 0, 0)),
        out_shape=jax.ShapeDtypeStruct((m // tk, n, tk), BF16),
        compiler_params=_params("parallel"),
        name=name,
    )(wt, a)


ATTN_ONES_ROWS = 16
ALIBI_PIECES = 5


def _alibi_tables(t):
    slopes = np.exp2(-8.0 * np.arange(1, DA_HEADS + 1, dtype=np.float64) / DA_HEADS)
    ramp = float(LOG2E) * slopes[:, None] * np.arange(t, dtype=np.float64)[None, :]

    def pieces(v):
        out = []
        for _ in range(ALIBI_PIECES):
            piece = v.astype(np.float32).astype(BF16).astype(np.float64)
            out.append(piece)
            v = v - piece
        assert not v.any()
        return out

    ones = np.ones_like(ramp)
    pad = [np.zeros_like(ramp)] * (DA_V_DIM - 2 * ALIBI_PIECES)
    key_side = np.stack(pieces(ramp) + [ones] * ALIBI_PIECES + pad, axis=-1)
    query_side = np.stack([ones] * ALIBI_PIECES + pieces(-ramp) + pad, axis=-1)
    return jnp.asarray(key_side, BF16), jnp.asarray(query_side, BF16)


def _attn_kernel(slopes_ref, lq1_ref, lk1_ref, lq2_ref, lk2_ref, sw_ref, q_ref, qb_ref, k_ref, kb_ref, vt_ref,
                 causal_ref, o_ref, sa_ref, sb_ref, m_ref, acc_ref, *, t, lam_init):
    h = pl.program_id(1)
    i = pl.program_id(2)
    slope_log2 = slopes_ref[h] * LOG2E

    q = q_ref[...]
    lane = lax.broadcasted_iota(jnp.int32, q.shape, 1)
    zero = jnp.zeros_like(q)
    q_bias = qb_ref[0]
    q_maps = (jnp.concatenate([jnp.where(lane < DA_QK_DIM, q, zero), q_bias], axis=1),
              jnp.concatenate([jnp.where(lane >= DA_QK_DIM, q, zero), q_bias], axis=1))
    k_bias = kb_ref[0]
    ones_rows = jnp.ones((ATTN_ONES_ROWS, t), BF16)

    m_ref[...] = jnp.full(m_ref.shape, NEG_INF, F32)
    acc_ref[...] = jnp.zeros(acc_ref.shape, F32)

    def scores(j, s_ref):
        k_aug = jnp.concatenate([k_ref[pl.ds(pl.multiple_of(j * t, t), t), :], k_bias], axis=1)
        for mp in range(2):
            s_ref[mp] = _dot_nt(k_aug, q_maps[mp])

    def consume(j, s_ref, diagonal=False):
        vt_aug = jnp.concatenate([vt_ref[j], ones_rows], axis=0)
        shift = slope_log2 * ((j - i) * t).astype(F32)
        for mp in range(2):
            s = s_ref[mp]
            if diagonal:
                s = s + causal_ref[...]
            m_old = m_ref[mp]
            m_new = jnp.maximum(m_old, jnp.max(s, axis=0, keepdims=True) + shift)
            p = jnp.exp2(s - (m_new - shift))
            acc_ref[mp] = jnp.exp2(m_old - m_new) * acc_ref[mp] + _dot(vt_aug, p.astype(BF16))
            m_ref[mp] = m_new

    scores(0, sa_ref)

    def tile_pair(jj, carry):
        j = 2 * jj
        scores(j + 1, sb_ref)
        consume(j, sa_ref)
        scores(j + 2, sa_ref)
        consume(j + 1, sb_ref)
        return carry

    lax.fori_loop(0, i // 2, tile_pair, 0)

    @pl.when(i % 2 == 0)
    def _():
        consume(i, sa_ref, diagonal=True)

    @pl.when(i % 2 == 1)
    def _():
        scores(i, sb_ref)
        consume(i - 1, sa_ref)
        consume(i, sb_ref, diagonal=True)

    lam = (jnp.exp(jnp.sum(lq1_ref[...] * lk1_ref[...], axis=-1, keepdims=True))
           - jnp.exp(jnp.sum(lq2_ref[...] * lk2_ref[...], axis=-1, keepdims=True)) + lam_init)
    a1 = acc_ref[0]
    a2 = acc_ref[1]
    o_t = (a1[:DA_V_DIM] / a1[DA_V_DIM:DA_V_DIM + 1]
           - lam * (a2[:DA_V_DIM] / a2[DA_V_DIM:DA_V_DIM + 1]))
    o = o_t.T
    o = o * lax.rsqrt(jnp.mean(o * o, axis=-1, keepdims=True) + LN_EPS)
    o_ref[...] = (o * sw_ref[...] * (1.0 - lam_init)).astype(o_ref.dtype)


def _diff_attention(qk, vtb, slopes, lq1, lk1, lq2, lk2, subln_w, batch, seq, lam_init, t):
    n_tok = qk.shape[0]
    nq = seq // t
    key_bias, query_bias = _alibi_tables(t)
    pos = np.arange(t)
    causal = jnp.asarray(np.where(pos[:, None] <= pos[None, :], 0.0, NEG_INF), F32)
    vec = pl.BlockSpec((1, DA_QK_DIM), lambda b, h, i: (0, 0))
    bias_spec = pl.BlockSpec((1, t, DA_V_DIM), lambda b, h, i: (h, 0, 0))
    return pl.pallas_call(
        functools.partial(_attn_kernel, t=t, lam_init=lam_init),
        grid=(batch, DA_HEADS, nq),
        in_specs=[
            pl.BlockSpec(memory_space=pltpu.SMEM),
            vec, vec, vec, vec,
            pl.BlockSpec((1, DA_V_DIM), lambda b, h, i: (0, 0)),
            pl.BlockSpec((t, DA_V_DIM), lambda b, h, i: (b * nq + i, h)),
            bias_spec,
            pl.BlockSpec((seq, DA_V_DIM), lambda b, h, i: (b, DA_HEADS + h)),
            bias_spec,
            pl.BlockSpec((nq, DA_V_DIM, t), lambda b, h, i: (b, h, 0)),
            pl.BlockSpec((t, t), lambda b, h, i: (0, 0)),
        ],
        out_specs=pl.BlockSpec((t, DA_V_DIM), lambda b, h, i: (b * nq + i, h)),
        out_shape=jax.ShapeDtypeStruct((n_tok, DA_WIDTH), BF16),
        scratch_shapes=[
            pltpu.VMEM((2, t, t), F32),
            pltpu.VMEM((2, t, t), F32),
            pltpu.VMEM((2, 1, t), F32),
            pltpu.VMEM((2, DA_V_DIM + ATTN_ONES_ROWS, t), F32),
        ],
        compiler_params=_params("parallel", "parallel", "arbitrary"),
        name="diff_attention",
    )(slopes, lq1, lk1, lq2, lk2, subln_w, qk, query_bias, qk, key_bias, vtb, causal)


def _mix_kernel(x_ref, xh_ref, o_ref, wp_ref, wgp_ref, wga_ref, poolw_ref, pscale_ref, wbp_ref, wba_ref,
                wout_ref, g_ref, b_ref, out_ref, e_ref, *, tm, seq):
    i = pl.program_id(0)
    start = (i * tm) % seq
    x = x_ref[...]
    xb = x.astype(BF16)

    p = _dot(xb, wp_ref[...])
    p_halo = _dot(xh_ref[...].astype(BF16), wp_ref[...])
    e_ref[0:POOL_HALO, :] = jnp.where(start == 0, 0.0, p_halo)
    e_ref[POOL_HALO:, :] = p

    pos = (start + 1 + lax.broadcasted_iota(jnp.int32, (tm, 1), 0)).astype(F32)
    ys = []
    for g, w in enumerate(POOL_WINDOWS):
        cols = slice(g * POOL_GROUP_DIM, (g + 1) * POOL_GROUP_DIM)
        acc = e_ref[POOL_HALO:POOL_HALO + tm, cols]
        for back in range(1, w):
            acc = acc + e_ref[POOL_HALO - back:POOL_HALO - back + tm, cols]
        pooled = acc / jnp.minimum(pos, float(w)) - e_ref[POOL_HALO:POOL_HALO + tm, cols]
        ys.append(_dot(pooled.astype(BF16), poolw_ref[g]))
    y = jnp.concatenate(ys, axis=-1) * pscale_ref[...]
    y_pool = _dot(y.astype(BF16), wbp_ref[...])

    gate_p = _dot(xb, wgp_ref[...])
    gate_a = _dot(xb, wga_ref[...])
    y_attn = _dot(o_ref[...], wba_ref[...])
    merged = jax.nn.sigmoid(gate_p) * y_pool + jax.nn.sigmoid(gate_a) * y_attn
    z = DN_ALPHA * x + _dot(merged.astype(BF16), wout_ref[...])
    out_ref[...] = _layer_norm(z, g_ref[...], b_ref[...])


def _mix_ln1(x2d, o, w_p, w_gp, w_ga, pool_w, pool_scale, w_bp, w_ba, w_out, g, b, seq, tm):
    n_tok = x2d.shape[0]
    halo_blocks = tm // POOL_HALO

    def full(a):
        return pl.BlockSpec(a.shape, lambda i: (0,) * a.ndim)

    return pl.pallas_call(
        functools.partial(_mix_kernel, tm=tm, seq=seq),
        grid=(n_tok // tm,),
        in_specs=[
            pl.BlockSpec((tm, D_MODEL), lambda i: (i, 0)),
            pl.BlockSpec((POOL_HALO, D_MODEL), lambda i: (jnp.maximum(i * halo_blocks - 1, 0), 0)),
            pl.BlockSpec((tm, DA_WIDTH), lambda i: (i, 0)),
            full(w_p), full(w_gp), full(w_ga), full(pool_w), full(pool_scale), full(w_bp), full(w_ba),
            full(w_out), full(g), full(b),
        ],
        out_specs=pl.BlockSpec((tm, D_MODEL), lambda i: (i, 0)),
        out_shape=jax.ShapeDtypeStruct((n_tok, D_MODEL), F32),
        scratch_shapes=[pltpu.VMEM((tm + POOL_HALO, POOL_WIDTH), F32)],
        compiler_params=_params("parallel"),
        name="mix_ln1",
    )(x2d, x2d, o, w_p, w_gp, w_ga, pool_w, pool_scale, w_bp, w_ba, w_out, g, b)


def _xattn_kernel(x_ref, kv_ref, wq_ref, wo_ref, g_ref, b_ref, out_ref):
    x = x_ref[...]
    q = (_dot(x.astype(BF16), wq_ref[...]) * (XA_HEAD_DIM ** -0.5)).astype(BF16)
    kv = kv_ref[0]
    outs = []
    for h in range(XA_HEADS):
        cols = slice(h * XA_HEAD_DIM, (h + 1) * XA_HEAD_DIM)
        s = _dot_nt(q[:, cols], kv[:, cols])
        p = jnp.exp(s - jnp.max(s, axis=-1, keepdims=True))
        l = jnp.sum(p, axis=-1, keepdims=True)
        v = kv[:, D_MODEL + h * XA_HEAD_DIM:D_MODEL + (h + 1) * XA_HEAD_DIM]
        outs.append(_dot(p.astype(BF16), v) / l)
    o = jnp.concatenate(outs, axis=-1)
    z = DN_ALPHA * x + _dot(o.astype(BF16), wo_ref[...])
    out_ref[...] = _layer_norm(z, g_ref[...], b_ref[...])


def _xattn_ln2(x1, kvm, w_cq, w_co, g, b, seq, tm):
    n_tok = x1.shape[0]
    n_mem = kvm.shape[1]

    def full(a):
        return pl.BlockSpec(a.shape, lambda i: (0,) * a.ndim)

    return pl.pallas_call(
        _xattn_kernel,
        grid=(n_tok // tm,),
        in_specs=[
            pl.BlockSpec((tm, D_MODEL), lambda i: (i, 0)),
            pl.BlockSpec((1, n_mem, 2 * D_MODEL), lambda i: ((i * tm) // seq, 0, 0)),
            full(w_cq), full(w_co), full(g), full(b),
        ],
        out_specs=pl.BlockSpec((tm, D_MODEL), lambda i: (i, 0)),
        out_shape=jax.ShapeDtypeStruct((n_tok, D_MODEL), F32),
        compiler_params=_params("parallel"),
        name="xattn_ln2",
    )(x1, kvm, w_cq, w_co, g, b)


def _top_values(s, n, with_rank):
    vals = []
    rank = jnp.full(s.shape, float(n), F32)
    cur = s
    for a in range(n):
        m = jnp.max(cur, axis=0, keepdims=True)
        vals.append(m)
        hit = cur == m
        if with_rank:
            rank = jnp.where(hit, float(a), rank)
        cur = jnp.where(hit, -jnp.inf, cur)
    return vals, rank


def _route_kernel(x_ref, wq_ref, keys_ref, rank2_ref, e2_ref, count_ref, d_ref, v1_ref, v2_ref):
    k = PEER_TOPK
    q_t = _dot_nt(wq_ref[...], x_ref[...].astype(BF16)).astype(BF16)
    s1 = _dot(keys_ref[0, 0], q_t[:PEER_HALF])
    s2 = _dot(keys_ref[0, 1], q_t[PEER_HALF:])

    v1, _ = _top_values(s1, k, False)
    v2, rank2 = _top_values(s2, k, True)
    for a in range(k):
        v1_ref[a:a + 1, :] = v1[a]
        v2_ref[a:a + 1, :] = v2[a]
    v1_all = v1_ref[...]

    cand = jnp.concatenate([v1_all + v2[0]] + [v1_all[:k // 2] + v2[b] for b in range(1, k // 2)]
                           + [v2_ref[k // 2:, :] + v1[0]], axis=0)
    m0 = v1[0] + v2[0]
    tau = jnp.full(m0.shape, jnp.inf, F32)
    seen = jnp.zeros(m0.shape, F32)
    z = jnp.zeros(m0.shape, F32)
    cur = cand
    for _ in range(k):
        m = jnp.max(cur, axis=0, keepdims=True)
        hit = cur == m
        cnt = jnp.sum(hit.astype(F32), axis=0, keepdims=True)
        open_ = seen < float(k)
        z = z + jnp.where(open_, cnt * jnp.exp(m - m0), 0.0)
        seen = seen + cnt
        tau = jnp.where(open_, m, tau)
        cur = jnp.where(hit, -jnp.inf, cur)

    count = jnp.zeros(s1.shape, F32)
    for b in range(k):
        count = count + jnp.where(s1 + v2[b] >= tau, 1.0, 0.0)

    rank2_ref[0] = rank2.astype(BF16)
    e2_ref[0] = jnp.exp(s2 - v2[0]).astype(BF16)
    count_ref[0] = count
    d_ref[0] = jnp.exp(s1 - v1[0]) / z


def _peer_route(x2, wq_t, keys, tm):
    n_tok = x2.shape[0]
    out_spec = pl.BlockSpec((1, PEER_N_KEYS, tm), lambda i, h: (h, 0, i))
    out_shape = [jax.ShapeDtypeStruct((PEER_HEADS, PEER_N_KEYS, n_tok), dt) for dt in (BF16, BF16, F32, F32)]
    return pl.pallas_call(
        _route_kernel,
        grid=(n_tok // tm, PEER_HEADS),
        in_specs=[
            pl.BlockSpec((tm, D_MODEL), lambda i, h: (i, 0)),
            pl.BlockSpec((PEER_QUERY_DIM, D_MODEL), lambda i, h: (h, 0)),
            pl.BlockSpec((1, 2, PEER_N_KEYS, PEER_HALF), lambda i, h: (h, 0, 0, 0)),
        ],
        out_specs=[out_spec] * 4,
        out_shape=out_shape,
        scratch_shapes=[pltpu.VMEM((PEER_TOPK, tm), F32), pltpu.VMEM((PEER_TOPK, tm), F32)],
        compiler_params=_params("parallel", "parallel"),
        name="peer_route",
    )(x2, wq_t, keys)


def _peer_kernel(x_ref, u_ref, vt_ref, rank2_ref, e2_ref, count_ref, d_ref, g_ref, b_ref, out_ref,
                 xb_ref, acc_ref, a_ref, *, chunks, rows_per_chunk):
    e = pl.program_id(1)

    @pl.when(e == 0)
    def _():
        xb_ref[...] = x_ref[...].astype(BF16)
        acc_ref[...] = jnp.zeros(acc_ref.shape, F32)

    xb = xb_ref[...]
    chunk = rows_per_chunk * PEER_N_KEYS
    tb = xb.shape[0]

    def sublane_rows(ref, h, row):
        tile = jnp.broadcast_to(ref[h, row:row + 1, :], (16, tb)).astype(BF16)
        return jnp.concatenate([tile] * (PEER_N_KEYS // 16), axis=0)

    def activations(c):
        return _dot_nt(u_ref[c * chunk:(c + 1) * chunk, :], xb)

    a_ref[0] = activations(0)
    for c in range(chunks):
        if c + 1 < chunks:
            a_ref[(c + 1) % 2] = activations(c + 1)
        parts = []
        for r in range(rows_per_chunk):
            row = c * rows_per_chunk + r
            a = a_ref[c % 2, r * PEER_N_KEYS:(r + 1) * PEER_N_KEYS, :]
            half = 0.5 * a
            act = half + half * lax.erf(a * SQRT_HALF)
            gate = None
            for h in range(PEER_HEADS):
                cnt = sublane_rows(count_ref, h, row)
                d = sublane_rows(d_ref, h, row)
                term = jnp.where(rank2_ref[h] < cnt, e2_ref[h], jnp.zeros((), BF16)) * d
                gate = term if gate is None else gate + term
            parts.append(gate * act.astype(BF16))
        w = jnp.concatenate(parts, axis=0)
        acc_ref[...] += _dot(vt_ref[:, c * chunk:(c + 1) * chunk], w)

    @pl.when(e == pl.num_programs(1) - 1)
    def _():
        z = DN_ALPHA * x_ref[...] + acc_ref[...].T
        out_ref[...] = _layer_norm(z, g_ref[...], b_ref[...])


def _peer_ln3(x2, u, vt, rank2, e2, count, d, g, b, tb, te, chunk):
    n_tok = x2.shape[0]
    rows_per_tile = te // PEER_N_KEYS
    key_spec = pl.BlockSpec((PEER_HEADS, PEER_N_KEYS, tb), lambda t, e: (0, 0, t))
    row_spec = pl.BlockSpec((PEER_HEADS, rows_per_tile, tb), lambda t, e: (0, e, t))
    vec = pl.BlockSpec((1, D_MODEL), lambda t, e: (0, 0))
    return pl.pallas_call(
        functools.partial(_peer_kernel, chunks=te // chunk, rows_per_chunk=chunk // PEER_N_KEYS),
        grid=(n_tok // tb, PEER_N_EXPERTS // te),
        in_specs=[
            pl.BlockSpec((tb, D_MODEL), lambda t, e: (t, 0)),
            pl.BlockSpec((te, D_MODEL), lambda t, e: (e, 0)),
            pl.BlockSpec((D_MODEL, te), lambda t, e: (0, e)),
            key_spec, key_spec, row_spec, row_spec,
            vec, vec,
        ],
        out_specs=pl.BlockSpec((tb, D_MODEL), lambda t, e: (t, 0)),
        out_shape=jax.ShapeDtypeStruct((n_tok, D_MODEL), F32),
        scratch_shapes=[
            pltpu.VMEM((tb, D_MODEL), BF16),
            pltpu.VMEM((D_MODEL, tb), F32),
            pltpu.VMEM((2, chunk, tb), F32),
        ],
        compiler_params=_params("parallel", "arbitrary"),
        name="peer_dense",
    )(x2, u, vt, rank2, e2, count, d, g, b)


def kernel(x, mem, w_in, pool_w, pool_scale, w_br_pool, lambda_q1, lambda_k1, lambda_q2, lambda_k2, subln_w, w_br_attn, w_out, ln1_g, ln1_b, w_cq, w_ckv, w_co, ln2_g, ln2_b, w_pq, sub_keys, expert_u, expert_v, ln3_g, ln3_b):
    batch, seq, _ = x.shape
    n_tok = batch * seq
    n_mem = mem.shape[1]
    attn_tile = 512
    slopes = jnp.exp2(-8.0 * jnp.arange(1, DA_HEADS + 1, dtype=F32) / DA_HEADS)

    def row(v):
        return v.reshape(1, -1).astype(F32)

    for l in range(DEPTH):
        lam_init = 0.8 - 0.6 * math.exp(-0.3 * l)
        x2d = x.reshape(n_tok, D_MODEL)
        w_in_b = w_in[l].astype(BF16)
        q0 = POOL_WIDTH
        k0 = q0 + DA_WIDTH
        v0 = k0 + DA_WIDTH
        gp0 = v0 + DA_WIDTH
        ga0 = gp0 + D_MODEL

        qk_scale = jnp.concatenate([jnp.full((1, DA_WIDTH), LOG2E * DA_QK_DIM ** -0.5, F32),
                                    jnp.ones((1, DA_WIDTH), F32)], axis=1)
        qk = _proj(x2d, w_in_b, q0 // 512, qk_scale, BF16, 1024, 512, "proj_qk")
        vtb = _proj_t_blocked(x2d, w_in_b[:, v0:gp0].T, attn_tile, "proj_vt")
        o = _diff_attention(qk, vtb, slopes, row(lambda_q1[l]), row(lambda_k1[l]), row(lambda_q2[l]),
                            row(lambda_k2[l]), row(subln_w[l]), batch, seq, lam_init, attn_tile)

        x1 = _mix_ln1(x2d, o, w_in_b[:, :POOL_WIDTH], w_in_b[:, gp0:ga0], w_in_b[:, ga0:],
                      pool_w[l].astype(BF16), row(pool_scale[l]), w_br_pool[l].astype(BF16),
                      w_br_attn[l].astype(BF16), w_out[l].astype(BF16), row(ln1_g[l]), row(ln1_b[l]), seq, 256)

        kvm = _proj(mem.reshape(batch * n_mem, D_MODEL), w_ckv[l].astype(BF16), 0, jnp.ones((1, 2 * D_MODEL), F32),
                    BF16, batch * n_mem, 512, "proj_mem_kv").reshape(batch, n_mem, 2 * D_MODEL)
        x2 = _xattn_ln2(x1, kvm, w_cq[l].astype(BF16), w_co[l].astype(BF16), row(ln2_g[l]), row(ln2_b[l]), seq, 256)

        rank2, e2, count, d = _peer_route(x2, w_pq[l].T.astype(BF16), sub_keys[l].astype(BF16), 256)
        x3 = _peer_ln3(x2, expert_u[l].astype(BF16), expert_v[l].T.astype(BF16), rank2, e2, count, d,
                       row(ln3_g[l]), row(ln3_b[l]), 512, 1024, 256)
        x = x3.reshape(batch, seq, D_MODEL)
    return x
```

```python
import functools
import math

import jax
import jax.numpy as jnp
import numpy as np
from jax import lax
from jax.experimental import pallas as pl
from jax.experimental.pallas import tpu as pltpu

F32 = jnp.float32
BF16 = jnp.bfloat16

D_MODEL = 1024
POOL_WINDOWS = (2, 4, 8, 16)
POOL_GROUPS = len(POOL_WINDOWS)
POOL_WIDTH = D_MODEL // 2
POOL_GROUP_DIM = POOL_WIDTH // POOL_GROUPS
POOL_HALO = max(POOL_WINDOWS)
DA_HEADS = 8
DA_QK_DIM = 64
DA_V_DIM = 2 * DA_QK_DIM
DA_WIDTH = DA_HEADS * DA_V_DIM
XA_HEADS = 4
XA_HEAD_DIM = D_MODEL // XA_HEADS
PEER_HEADS = 8
PEER_N_KEYS = 128
PEER_N_EXPERTS = PEER_N_KEYS * PEER_N_KEYS
PEER_QUERY_DIM = 128
PEER_HALF = PEER_QUERY_DIM // 2
PEER_TOPK = 16
DEPTH = 1
DN_ALPHA = (2 * DEPTH) ** 0.25
LN_EPS = 1e-5
NEG_INF = -1e30
SQRT_HALF = math.sqrt(0.5)
LOG2E = float(np.float32(math.log2(math.e)))

VMEM_LIMIT_BYTES = 52 * 1024 * 1024

_NT = (((1,), (1,)), ((), ()))


def _params(*sem):
    return pltpu.CompilerParams(dimension_semantics=sem, vmem_limit_bytes=VMEM_LIMIT_BYTES)


def _dot(a, b):
    return jnp.dot(a, b, preferred_element_type=F32)


def _dot_nt(a, b):
    return lax.dot_general(a, b, _NT, preferred_element_type=F32)


def _layer_norm(z, g, b):
    mu = jnp.mean(z, axis=-1, keepdims=True)
    zc = z - mu
    var = jnp.mean(zc * zc, axis=-1, keepdims=True)
    return zc * lax.rsqrt(var + LN_EPS) * g + b


def _proj_kernel(a_ref, w_ref, scale_ref, o_ref):
    o_ref[...] = (_dot(a_ref[...].astype(BF16), w_ref[...]) * scale_ref[...]).astype(o_ref.dtype)


def _proj(a, w, col_block0, col_scale, out_dtype, tm, tn, name):
    m, k = a.shape
    n_cols = col_scale.shape[1]
    return pl.pallas_call(
        _proj_kernel,
        grid=(m // tm, n_cols // tn),
        in_specs=[
            pl.BlockSpec((tm, k), lambda i, j: (i, 0)),
            pl.BlockSpec((k, tn), lambda i, j: (0, j + col_block0)),
            pl.BlockSpec((1, tn), lambda i, j: (0, j)),
        ],
        out_specs=pl.BlockSpec((tm, tn), lambda i, j: (i, j)),
        out_shape=jax.ShapeDtypeStruct((m, n_cols), out_dtype),
        compiler_params=_params("parallel", "parallel"),
        name=name,
    )(a, w, col_scale)


def _proj_t_kernel(wt_ref, a_ref, o_ref):
    o_ref[0] = _dot_nt(wt_ref[...], a_ref[...].astype(BF16)).astype(o_ref.dtype)


def _proj_t_blocked(a, wt, tk, name):
    m, k = a.shape
    n = wt.shape[0]
    return pl.pallas_call(
        _proj_t_kernel,
        grid=(m // tk,),
        in_specs=[
            pl.BlockSpec((n, k), lambda j: (0, 0)),
            pl.BlockSpec((tk, k), lambda j: (j, 0)),
        ],
        out_specs=pl.BlockSpec((1, n, tk), lambda j: (j, 0, 0)),
        out_shape=jax.ShapeDtypeStruct((m // tk, n, tk), BF16),
        compiler_params=_params("parallel"),
        name=name,
    )(wt, a)


ATTN_ONES_ROWS = 16
ALIBI_PIECES = 5


def _alibi_tables(t):
    slopes = np.exp2(-8.0 * np.arange(1, DA_HEADS + 1, dtype=np.float64) / DA_HEADS)
    ramp = float(LOG2E) * slopes[:, None] * np.arange(t, dtype=np.float64)[None, :]

    def pieces(v):
        out = []
        for _ in range(ALIBI_PIECES):
            piece = v.astype(np.float32).astype(BF16).astype(np.float64)
            out.append(piece)
            v = v - piece
        assert not v.any()
        return out

    ones = np.ones_like(ramp)
    pad = [np.zeros_like(ramp)] * (DA_V_DIM - 2 * ALIBI_PIECES)
    key_side = np.stack(pieces(ramp) + [ones] * ALIBI_PIECES + pad, axis=-1)
    query_side = np.stack([ones] * ALIBI_PIECES + pieces(-ramp) + pad, axis=-1)
    return jnp.asarray(key_side, BF16), jnp.asarray(query_side, BF16)


def _attn_kernel(slopes_ref, lq1_ref, lk1_ref, lq2_ref, lk2_ref, sw_ref, q_ref, qb_ref, k_ref, kb_ref, vt_ref,
                 causal_ref, o_ref, sa_ref, sb_ref, m_ref, acc_ref, *, t, heads, lam_init):
    hg = pl.program_id(1)
    i = pl.program_id(2)
    slope_log2 = [slopes_ref[hg * heads + g] * LOG2E for g in range(heads)]

    lane = lax.broadcasted_iota(jnp.int32, (t, DA_V_DIM), 1)
    q_maps = []
    for g in range(heads):
        q = q_ref[:, g * DA_V_DIM:(g + 1) * DA_V_DIM]
        zero = jnp.zeros_like(q)
        q_maps += [jnp.concatenate([jnp.where(lane < DA_QK_DIM, q, zero), qb_ref[g]], axis=1),
                   jnp.concatenate([jnp.where(lane >= DA_QK_DIM, q, zero), qb_ref[g]], axis=1)]
    ones_rows = jnp.ones((ATTN_ONES_ROWS, t), BF16)

    m_ref[...] = jnp.full(m_ref.shape, NEG_INF, F32)
    acc_ref[...] = jnp.zeros(acc_ref.shape, F32)

    def scores(j, s_ref):
        rows = pl.ds(pl.multiple_of(j * t, t), t)
        for g in range(heads):
            k_aug = jnp.concatenate([k_ref[rows, g * DA_V_DIM:(g + 1) * DA_V_DIM], kb_ref[g]], axis=1)
            for mp in range(2):
                s_ref[2 * g + mp] = _dot_nt(k_aug, q_maps[2 * g + mp])

    def consume(j, s_ref, diagonal=False):
        for g in range(heads):
            vt_aug = jnp.concatenate([vt_ref[j, g * DA_V_DIM:(g + 1) * DA_V_DIM, :], ones_rows], axis=0)
            shift = slope_log2[g] * ((j - i) * t).astype(F32)
            for mp in range(2):
                c = 2 * g + mp
                s = s_ref[c]
                if diagonal:
                    s = s + causal_ref[...]
                m_old = m_ref[c]
                m_new = jnp.maximum(m_old, jnp.max(s, axis=0, keepdims=True) + shift)
                p = jnp.exp2(s - (m_new - shift))
                acc_ref[c] = jnp.exp2(m_old - m_new) * acc_ref[c] + _dot(vt_aug, p.astype(BF16))
                m_ref[c] = m_new

    scores(0, sa_ref)

    def tile_pair(jj, carry):
        j = 2 * jj
        scores(j + 1, sb_ref)
        consume(j, sa_ref)
        scores(j + 2, sa_ref)
        consume(j + 1, sb_ref)
        return carry

    lax.fori_loop(0, i // 2, tile_pair, 0)

    @pl.when(i % 2 == 0)
    def _():
        consume(i, sa_ref, diagonal=True)

    @pl.when(i % 2 == 1)
    def _():
        scores(i, sb_ref)
        consume(i - 1, sa_ref)
        consume(i, sb_ref, diagonal=True)

    lam = (jnp.exp(jnp.sum(lq1_ref[...] * lk1_ref[...], axis=-1, keepdims=True))
           - jnp.exp(jnp.sum(lq2_ref[...] * lk2_ref[...], axis=-1, keepdims=True)) + lam_init)
    for g in range(heads):
        a1 = acc_ref[2 * g]
        a2 = acc_ref[2 * g + 1]
        o_t = (a1[:DA_V_DIM] / a1[DA_V_DIM:DA_V_DIM + 1]
               - lam * (a2[:DA_V_DIM] / a2[DA_V_DIM:DA_V_DIM + 1]))
        o = o_t.T
        o = o * lax.rsqrt(jnp.mean(o * o, axis=-1, keepdims=True) + LN_EPS)
        o_ref[:, g * DA_V_DIM:(g + 1) * DA_V_DIM] = (o * sw_ref[...] * (1.0 - lam_init)).astype(o_ref.dtype)


def _diff_attention(qk, vtb, slopes, lq1, lk1, lq2, lk2, subln_w, batch, seq, lam_init, t, heads):
    n_tok = qk.shape[0]
    nq = seq // t
    width = heads * DA_V_DIM
    key_bias, query_bias = _alibi_tables(t)
    pos = np.arange(t)
    causal = jnp.asarray(np.where(pos[:, None] <= pos[None, :], 0.0, NEG_INF), F32)
    vec = pl.BlockSpec((1, DA_QK_DIM), lambda b, h, i: (0, 0))
    bias_spec = pl.BlockSpec((heads, t, DA_V_DIM), lambda b, h, i: (h, 0, 0))
    maps = 2 * heads
    return pl.pallas_call(
        functools.partial(_attn_kernel, t=t, heads=heads, lam_init=lam_init),
        grid=(batch, DA_HEADS // heads, nq),
        in_specs=[
            pl.BlockSpec(memory_space=pltpu.SMEM),
            vec, vec, vec, vec,
            pl.BlockSpec((1, DA_V_DIM), lambda b, h, i: (0, 0)),
            pl.BlockSpec((t, width), lambda b, h, i: (b * nq + i, h)),
            bias_spec,
            pl.BlockSpec((seq, width), lambda b, h, i: (b, DA_HEADS // heads + h)),
            bias_spec,
            pl.BlockSpec((nq, width, t), lambda b, h, i: (b, h, 0)),
            pl.BlockSpec((t, t), lambda b, h, i: (0, 0)),
        ],
        out_specs=pl.BlockSpec((t, width), lambda b, h, i: (b * nq + i, h)),
        out_shape=jax.ShapeDtypeStruct((n_tok, DA_WIDTH), BF16),
        scratch_shapes=[
            pltpu.VMEM((maps, t, t), F32),
            pltpu.VMEM((maps, t, t), F32),
            pltpu.VMEM((maps, 1, t), F32),
            pltpu.VMEM((maps, DA_V_DIM + ATTN_ONES_ROWS, t), F32),
        ],
        compiler_params=_params("parallel", "parallel", "arbitrary"),
        name="diff_attention",
    )(slopes, lq1, lk1, lq2, lk2, subln_w, qk, query_bias, qk, key_bias, vtb, causal)


def _mix_kernel(x_ref, xh_ref, o_ref, wp_ref, wgp_ref, wga_ref, poolw_ref, pscale_ref, wbp_ref, wba_ref,
                wout_ref, g_ref, b_ref, out_ref, e_ref, *, tm, seq):
    i = pl.program_id(0)
    start = (i * tm) % seq
    x = x_ref[...]
    xb = x.astype(BF16)

    p = _dot(xb, wp_ref[...])
    p_halo = _dot(xh_ref[...].astype(BF16), wp_ref[...])
    e_ref[0:POOL_HALO, :] = jnp.where(start == 0, 0.0, p_halo)
    e_ref[POOL_HALO:, :] = p

    pos = (start + 1 + lax.broadcasted_iota(jnp.int32, (tm, 1), 0)).astype(F32)
    ys = []
    for g, w in enumerate(POOL_WINDOWS):
        cols = slice(g * POOL_GROUP_DIM, (g + 1) * POOL_GROUP_DIM)
        acc = e_ref[POOL_HALO:POOL_HALO + tm, cols]
        for back in range(1, w):
            acc = acc + e_ref[POOL_HALO - back:POOL_HALO - back + tm, cols]
        pooled = acc / jnp.minimum(pos, float(w)) - e_ref[POOL_HALO:POOL_HALO + tm, cols]
        ys.append(_dot(pooled.astype(BF16), poolw_ref[g]))
    y = jnp.concatenate(ys, axis=-1) * pscale_ref[...]
    y_pool = _dot(y.astype(BF16), wbp_ref[...])

    gate_p = _dot(xb, wgp_ref[...])
    gate_a = _dot(xb, wga_ref[...])
    y_attn = _dot(o_ref[...], wba_ref[...])
    merged = jax.nn.sigmoid(gate_p) * y_pool + jax.nn.sigmoid(gate_a) * y_attn
    z = DN_ALPHA * x + _dot(merged.astype(BF16), wout_ref[...])
    out_ref[...] = _layer_norm(z, g_ref[...], b_ref[...])


def _mix_ln1(x2d, o, w_p, w_gp, w_ga, pool_w, pool_scale, w_bp, w_ba, w_out, g, b, seq, tm):
    n_tok = x2d.shape[0]
    halo_blocks = tm // POOL_HALO

    def full(a):
        return pl.BlockSpec(a.shape, lambda i: (0,) * a.ndim)

    return pl.pallas_call(
        functools.partial(_mix_kernel, tm=tm, seq=seq),
        grid=(n_tok // tm,),
        in_specs=[
            pl.BlockSpec((tm, D_MODEL), lambda i: (i, 0)),
            pl.BlockSpec((POOL_HALO, D_MODEL), lambda i: (jnp.maximum(i * halo_blocks - 1, 0), 0)),
            pl.BlockSpec((tm, DA_WIDTH), lambda i: (i, 0)),
            full(w_p), full(w_gp), full(w_ga), full(pool_w), full(pool_scale), full(w_bp), full(w_ba),
            full(w_out), full(g), full(b),
        ],
        out_specs=pl.BlockSpec((tm, D_MODEL), lambda i: (i, 0)),
        out_shape=jax.ShapeDtypeStruct((n_tok, D_MODEL), F32),
        scratch_shapes=[pltpu.VMEM((tm + POOL_HALO, POOL_WIDTH), F32)],
        compiler_params=_params("parallel"),
        name="mix_ln1",
    )(x2d, x2d, o, w_p, w_gp, w_ga, pool_w, pool_scale, w_bp, w_ba, w_out, g, b)


def _xattn_kernel(x_ref, kv_ref, wq_ref, wo_ref, g_ref, b_ref, out_ref):
    x = x_ref[...]
    q = (_dot(x.astype(BF16), wq_ref[...]) * (XA_HEAD_DIM ** -0.5)).astype(BF16)
    kv = kv_ref[0]
    outs = []
    for h in range(XA_HEADS):
        cols = slice(h * XA_HEAD_DIM, (h + 1) * XA_HEAD_DIM)
        s = _dot_nt(q[:, cols], kv[:, cols])
        p = jnp.exp(s - jnp.max(s, axis=-1, keepdims=True))
        l = jnp.sum(p, axis=-1, keepdims=True)
        v = kv[:, D_MODEL + h * XA_HEAD_DIM:D_MODEL + (h + 1) * XA_HEAD_DIM]
        outs.append(_dot(p.astype(BF16), v) / l)
    o = jnp.concatenate(outs, axis=-1)
    z = DN_ALPHA * x + _dot(o.astype(BF16), wo_ref[...])
    out_ref[...] = _layer_norm(z, g_ref[...], b_ref[...])


def _xattn_ln2(x1, kvm, w_cq, w_co, g, b, seq, tm):
    n_tok = x1.shape[0]
    n_mem = kvm.shape[1]

    def full(a):
        return pl.BlockSpec(a.shape, lambda i: (0,) * a.ndim)

    return pl.pallas_call(
        _xattn_kernel,
        grid=(n_tok // tm,),
        in_specs=[
            pl.BlockSpec((tm, D_MODEL), lambda i: (i, 0)),
            pl.BlockSpec((1, n_mem, 2 * D_MODEL), lambda i: ((i * tm) // seq, 0, 0)),
            full(w_cq), full(w_co), full(g), full(b),
        ],
        out_specs=pl.BlockSpec((tm, D_MODEL), lambda i: (i, 0)),
        out_shape=jax.ShapeDtypeStruct((n_tok, D_MODEL), F32),
        compiler_params=_params("parallel"),
        name="xattn_ln2",
    )(x1, kvm, w_cq, w_co, g, b)


SUBLANES = 8


def _sort_network(n):
    def merge(lo, hi, r):
        step = 2 * r
        if step < hi - lo:
            yield from merge(lo, hi, step)
            yield from merge(lo + r, hi, step)
            yield from ((i, i + r) for i in range(lo + r, hi - r, step))
        else:
            yield (lo, lo + r)

    def sort(lo, hi):
        if hi - lo >= 1:
            mid = lo + (hi - lo) // 2
            yield from sort(lo, mid)
            yield from sort(mid + 1, hi)
            yield from merge(lo, hi, 1)

    return tuple(sort(0, n - 1))


def _exchange(v, i, j):
    if v[j] is None:
        return
    if v[i] is None:
        v[i], v[j] = v[j], None
        return
    v[i], v[j] = jnp.maximum(v[i], v[j]), jnp.minimum(v[i], v[j])


def _top_sorted(blocks, k):
    v = list(blocks) + [None] * (k - len(blocks))
    for i, j in _sort_network(k):
        _exchange(v, i, j)
    shift = SUBLANES // 2
    while shift >= 1:
        other = [None if a is None else pltpu.roll(a, shift, 0) for a in v]
        v = [b if a is None else (a if b is None else jnp.maximum(a, b)) for a, b in zip(v, reversed(other))]
        stride = k // 2
        while stride >= 1:
            for i in range(k):
                if not i & stride:
                    _exchange(v, i, i + stride)
            stride //= 2
        shift //= 2
    return v


def _route_kernel(x_ref, wq_ref, keys_ref, rank2_ref, e2_ref, count_ref, d_ref):
    k = PEER_TOPK
    q_t = _dot_nt(wq_ref[...], x_ref[...].astype(BF16)).astype(BF16)
    s1 = _dot(keys_ref[0, 0], q_t[:PEER_HALF])
    s2 = _dot(keys_ref[0, 1], q_t[PEER_HALF:])
    n_blocks = PEER_N_KEYS // SUBLANES
    s1_blocks = [s1[r * SUBLANES:(r + 1) * SUBLANES] for r in range(n_blocks)]
    s2_blocks = [s2[r * SUBLANES:(r + 1) * SUBLANES] for r in range(n_blocks)]
    v1 = _top_sorted(s1_blocks, k)
    v2 = _top_sorted(s2_blocks, k)

    row = lax.broadcasted_iota(jnp.int32, v1[0].shape, 0)

    def pack(vals):
        out = vals[0]
        for r in range(1, SUBLANES):
            out = jnp.where(row == r, vals[r], out)
        return out

    v1_lo, v1_hi, v2_hi = pack(v1[:SUBLANES]), pack(v1[SUBLANES:]), pack(v2[SUBLANES:])
    cand = ([v1_lo + v2[0], v1_hi + v2[0]] + [v1_lo + v2[b] for b in range(1, SUBLANES)] + [v2_hi + v1[0]])
    top = _top_sorted(cand, k)
    tau = top[k - 1]
    z = jnp.ones(tau.shape, F32)
    for c in top[1:]:
        z = z + jnp.exp(c - top[0])
    inv_z = 1.0 / z

    rank2, e2, count, d = [], [], [], []
    for blk1, blk2 in zip(s1_blocks, s2_blocks):
        r2 = jnp.full(blk2.shape, float(k), F32)
        cnt = jnp.zeros(blk1.shape, F32)
        for b in reversed(range(k)):
            r2 = jnp.where(blk2 >= v2[b], float(b), r2)
        for b in range(k):
            cnt = jnp.where(blk1 + v2[b] >= tau, float(b + 1), cnt)
        rank2.append(r2)
        count.append(cnt)
        e2.append(jnp.exp(blk2 - v2[0]))
        d.append(jnp.exp(blk1 - v1[0]) * inv_z)
    rank2_ref[0] = jnp.concatenate(rank2, axis=0).astype(BF16)
    e2_ref[0] = jnp.concatenate(e2, axis=0).astype(BF16)
    count_ref[0] = jnp.concatenate(count, axis=0)
    d_ref[0] = jnp.concatenate(d, axis=0)


def _peer_route(x2, wq_t, keys, tm):
    n_tok = x2.shape[0]
    out_spec = pl.BlockSpec((1, PEER_N_KEYS, tm), lambda i, h: (h, 0, i))
    out_shape = [jax.ShapeDtypeStruct((PEER_HEADS, PEER_N_KEYS, n_tok), dt) for dt in (BF16, BF16, F32, F32)]
    return pl.pallas_call(
        _route_kernel,
        grid=(n_tok // tm, PEER_HEADS),
        in_specs=[
            pl.BlockSpec((tm, D_MODEL), lambda i, h: (i, 0)),
            pl.BlockSpec((PEER_QUERY_DIM, D_MODEL), lambda i, h: (h, 0)),
            pl.BlockSpec((1, 2, PEER_N_KEYS, PEER_HALF), lambda i, h: (h, 0, 0, 0)),
        ],
        out_specs=[out_spec] * 4,
        out_shape=out_shape,
        compiler_params=_params("parallel", "parallel"),
        name="peer_route",
    )(x2, wq_t, keys)


def _peer_kernel(x_ref, u_ref, vt_ref, rank2_ref, e2_ref, count_ref, d_ref, g_ref, b_ref, out_ref,
                 xb_ref, acc_ref, a_ref, *, chunks, rows_per_chunk):
    e = pl.program_id(1)

    @pl.when(e == 0)
    def _():
        xb_ref[...] = x_ref[...].astype(BF16)
        acc_ref[...] = jnp.zeros(acc_ref.shape, F32)

    xb = xb_ref[...]
    chunk = rows_per_chunk * PEER_N_KEYS
    tb = xb.shape[0]

    def sublane_rows(ref, h, row):
        tile = jnp.broadcast_to(ref[h, row:row + 1, :], (16, tb)).astype(BF16)
        return jnp.concatenate([tile] * (PEER_N_KEYS // 16), axis=0)

    def activations(c):
        return _dot_nt(u_ref[c * chunk:(c + 1) * chunk, :], xb)

    a_ref[0] = activations(0)
    for c in range(chunks):
        if c + 1 < chunks:
            a_ref[(c + 1) % 2] = activations(c + 1)
        parts = []
        for r in range(rows_per_chunk):
            row = c * rows_per_chunk + r
            a = a_ref[c % 2, r * PEER_N_KEYS:(r + 1) * PEER_N_KEYS, :]
            half = 0.5 * a
            act = half + half * lax.erf(a * SQRT_HALF)
            gate = None
            for h in range(PEER_HEADS):
                cnt = sublane_rows(count_ref, h, row)
                d = sublane_rows(d_ref, h, row)
                term = jnp.where(rank2_ref[h] < cnt, e2_ref[h], jnp.zeros((), BF16)) * d
                gate = term if gate is None else gate + term
            parts.append(gate * act.astype(BF16))
        w = jnp.concatenate(parts, axis=0)
        acc_ref[...] += _dot(vt_ref[:, c * chunk:(c + 1) * chunk], w)

    @pl.when(e == pl.num_programs(1) - 1)
    def _():
        z = DN_ALPHA * x_ref[...] + acc_ref[...].T
        out_ref[...] = _layer_norm(z, g_ref[...], b_ref[...])


def _peer_ln3(x2, u, vt, rank2, e2, count, d, g, b, tb, te, chunk):
    n_tok = x2.shape[0]
    rows_per_tile = te // PEER_N_KEYS
    key_spec = pl.BlockSpec((PEER_HEADS, PEER_N_KEYS, tb), lambda t, e: (0, 0, t))
    row_spec = pl.BlockSpec((PEER_HEADS, rows_per_tile, tb), lambda t, e: (0, e, t))
    vec = pl.BlockSpec((1, D_MODEL), lambda t, e: (0, 0))
    return pl.pallas_call(
        functools.partial(_peer_kernel, chunks=te // chunk, rows_per_chunk=chunk // PEER_N_KEYS),
        grid=(n_tok // tb, PEER_N_EXPERTS // te),
        in_specs=[
            pl.BlockSpec((tb, D_MODEL), lambda t, e: (t, 0)),
            pl.BlockSpec((te, D_MODEL), lambda t, e: (e, 0)),
            pl.BlockSpec((D_MODEL, te), lambda t, e: (0, e)),
            key_spec, key_spec, row_spec, row_spec,
            vec, vec,
        ],
        out_specs=pl.BlockSpec((tb, D_MODEL), lambda t, e: (t, 0)),
        out_shape=jax.ShapeDtypeStruct((n_tok, D_MODEL), F32),
        scratch_shapes=[
            pltpu.VMEM((tb, D_MODEL), BF16),
            pltpu.VMEM((D_MODEL, tb), F32),
            pltpu.VMEM((2, chunk, tb), F32),
        ],
        compiler_params=_params("parallel", "arbitrary"),
        name="peer_dense",
    )(x2, u, vt, rank2, e2, count, d, g, b)


def kernel(x, mem, w_in, pool_w, pool_scale, w_br_pool, lambda_q1, lambda_k1, lambda_q2, lambda_k2, subln_w, w_br_attn, w_out, ln1_g, ln1_b, w_cq, w_ckv, w_co, ln2_g, ln2_b, w_pq, sub_keys, expert_u, expert_v, ln3_g, ln3_b):
    batch, seq, _ = x.shape
    n_tok = batch * seq
    n_mem = mem.shape[1]
    attn_tile = 512
    slopes = jnp.exp2(-8.0 * jnp.arange(1, DA_HEADS + 1, dtype=F32) / DA_HEADS)

    def row(v):
        return v.reshape(1, -1).astype(F32)

    for l in range(DEPTH):
        lam_init = 0.8 - 0.6 * math.exp(-0.3 * l)
        x2d = x.reshape(n_tok, D_MODEL)
        w_in_b = w_in[l].astype(BF16)
        q0 = POOL_WIDTH
        k0 = q0 + DA_WIDTH
        v0 = k0 + DA_WIDTH
        gp0 = v0 + DA_WIDTH
        ga0 = gp0 + D_MODEL

        qk_scale = jnp.concatenate([jnp.full((1, DA_WIDTH), LOG2E * DA_QK_DIM ** -0.5, F32),
                                    jnp.ones((1, DA_WIDTH), F32)], axis=1)
        qk = _proj(x2d, w_in_b, q0 // 512, qk_scale, BF16, 1024, 512, "proj_qk")
        vtb = _proj_t_blocked(x2d, w_in_b[:, v0:gp0].T, attn_tile, "proj_vt")
        o = _diff_attention(qk, vtb, slopes, row(lambda_q1[l]), row(lambda_k1[l]), row(lambda_q2[l]),
                            row(lambda_k2[l]), row(subln_w[l]), batch, seq, lam_init, attn_tile, 2)

        x1 = _mix_ln1(x2d, o, w_in_b[:, :POOL_WIDTH], w_in_b[:, gp0:ga0], w_in_b[:, ga0:],
                      pool_w[l].astype(BF16), row(pool_scale[l]), w_br_pool[l].astype(BF16),
                      w_br_attn[l].astype(BF16), w_out[l].astype(BF16), row(ln1_g[l]), row(ln1_b[l]), seq, 256)

        kvm = _proj(mem.reshape(batch * n_mem, D_MODEL), w_ckv[l].astype(BF16), 0, jnp.ones((1, 2 * D_MODEL), F32),
                    BF16, batch * n_mem, 512, "proj_mem_kv").reshape(batch, n_mem, 2 * D_MODEL)
        x2 = _xattn_ln2(x1, kvm, w_cq[l].astype(BF16), w_co[l].astype(BF16), row(ln2_g[l]), row(ln2_b[l]), seq, 256)

        rank2, e2, count, d = _peer_route(x2, w_pq[l].T.astype(BF16), sub_keys[l].astype(BF16), 512)
        x3 = _peer_ln3(x2, expert_u[l].astype(BF16), expert_v[l].T.astype(BF16), rank2, e2, count, d,
                       row(ln3_g[l]), row(ln3_b[l]), 512, 1024, 256)
        x = x3.reshape(batch, seq, D_MODEL)
    return x
```

```python
import functools
import math

import jax
import jax.numpy as jnp
import numpy as np
from jax import lax
from jax.experimental import pallas as pl
from jax.experimental.pallas import tpu as pltpu

F32 = jnp.float32
BF16 = jnp.bfloat16

D_MODEL = 1024
POOL_WINDOWS = (2, 4, 8, 16)
POOL_GROUPS = len(POOL_WINDOWS)
POOL_WIDTH = D_MODEL // 2
POOL_GROUP_DIM = POOL_WIDTH // POOL_GROUPS
POOL_HALO = max(POOL_WINDOWS)
DA_HEADS = 8
DA_QK_DIM = 64
DA_V_DIM = 2 * DA_QK_DIM
DA_WIDTH = DA_HEADS * DA_V_DIM
XA_HEADS = 4
XA_HEAD_DIM = D_MODEL // XA_HEADS
PEER_HEADS = 8
PEER_N_KEYS = 128
PEER_N_EXPERTS = PEER_N_KEYS * PEER_N_KEYS
PEER_QUERY_DIM = 128
PEER_HALF = PEER_QUERY_DIM // 2
PEER_TOPK = 16
DEPTH = 1
DN_ALPHA = (2 * DEPTH) ** 0.25
LN_EPS = 1e-5
NEG_INF = -1e30
SQRT_HALF = math.sqrt(0.5)
LOG2E = float(np.float32(math.log2(math.e)))

VMEM_LIMIT_BYTES = 52 * 1024 * 1024

_NT = (((1,), (1,)), ((), ()))


def _params(*sem):
    return pltpu.CompilerParams(dimension_semantics=sem, vmem_limit_bytes=VMEM_LIMIT_BYTES)


def _dot(a, b):
    return jnp.dot(a, b, preferred_element_type=F32)


def _dot_nt(a, b):
    return lax.dot_general(a, b, _NT, preferred_element_type=F32)


def _layer_norm(z, g, b):
    mu = jnp.mean(z, axis=-1, keepdims=True)
    zc = z - mu
    var = jnp.mean(zc * zc, axis=-1, keepdims=True)
    return zc * lax.rsqrt(var + LN_EPS) * g + b


def _proj_kernel(a_ref, w_ref, scale_ref, o_ref):
    o_ref[...] = (_dot(a_ref[...].astype(BF16), w_ref[...]) * scale_ref[...]).astype(o_ref.dtype)


def _proj(a, w, col_block0, col_scale, out_dtype, tm, tn, name):
    m, k = a.shape
    n_cols = col_scale.shape[1]
    return pl.pallas_call(
        _proj_kernel,
        grid=(m // tm, n_cols // tn),
        in_specs=[
            pl.BlockSpec((tm, k), lambda i, j: (i, 0)),
            pl.BlockSpec((k, tn), lambda i, j: (0, j + col_block0)),
            pl.BlockSpec((1, tn), lambda i, j: (0, j)),
        ],
        out_specs=pl.BlockSpec((tm, tn), lambda i, j: (i, j)),
        out_shape=jax.ShapeDtypeStruct((m, n_cols), out_dtype),
        compiler_params=_params("parallel", "parallel"),
        name=name,
    )(a, w, col_scale)


def _proj_t_kernel(wt_ref, a_ref, o_ref):
    o_ref[0] = _dot_nt(wt_ref[...], a_ref[...].astype(BF16)).astype(o_ref.dtype)


def _proj_t_blocked(a, wt, tk, name):
    m, k = a.shape
    n = wt.shape[0]
    return pl.pallas_call(
        _proj_t_kernel,
        grid=(m // tk,),
        in_specs=[
            pl.BlockSpec((n, k), lambda j: (0, 0)),
            pl.BlockSpec((tk, k), lambda j: (j, 0)),
        ],
        out_specs=pl.BlockSpec((1, n, tk), lambda j: (j, 0, 0)),
        out_shape=jax.ShapeDtypeStruct((m // tk, n, tk), BF16),
        compiler_params=_params("parallel"),
        name=name,
    )(wt, a)


ATTN_ONES_ROWS = 16
ALIBI_PIECES = 5


def _alibi_tables(t):
    slopes = np.exp2(-8.0 * np.arange(1, DA_HEADS + 1, dtype=np.float64) / DA_HEADS)
    ramp = float(LOG2E) * slopes[:, None] * np.arange(t, dtype=np.float64)[None, :]

    def pieces(v):
        out = []
        for _ in range(ALIBI_PIECES):
            piece = v.astype(np.float32).astype(BF16).astype(np.float64)
            out.append(piece)
            v = v - piece
        assert not v.any()
        return out

    ones = np.ones_like(ramp)
    pad = [np.zeros_like(ramp)] * (DA_V_DIM - 2 * ALIBI_PIECES)
    key_side = np.stack(pieces(ramp) + [ones] * ALIBI_PIECES + pad, axis=-1)
    query_side = np.stack([ones] * ALIBI_PIECES + pieces(-ramp) + pad, axis=-1)
    return jnp.asarray(key_side, BF16), jnp.asarray(query_side, BF16)


def _attn_kernel(slopes_ref, lq1_ref, lk1_ref, lq2_ref, lk2_ref, sw_ref, q_ref, qb_ref, k_ref, kb_ref, vt_ref,
                 causal_ref, o_ref, sa_ref, sb_ref, m_ref, acc_ref, *, t, heads, lam_init):
    hg = pl.program_id(1)
    i = pl.program_id(2)
    slope_log2 = [slopes_ref[hg * heads + g] * LOG2E for g in range(heads)]

    lane = lax.broadcasted_iota(jnp.int32, (t, DA_V_DIM), 1)
    q_maps = []
    for g in range(heads):
        q = q_ref[:, g * DA_V_DIM:(g + 1) * DA_V_DIM]
        zero = jnp.zeros_like(q)
        q_maps += [jnp.concatenate([jnp.where(lane < DA_QK_DIM, q, zero), qb_ref[g]], axis=1),
                   jnp.concatenate([jnp.where(lane >= DA_QK_DIM, q, zero), qb_ref[g]], axis=1)]
    ones_rows = jnp.ones((ATTN_ONES_ROWS, t), BF16)

    m_ref[...] = jnp.full(m_ref.shape, NEG_INF, F32)
    acc_ref[...] = jnp.zeros(acc_ref.shape, F32)

    def scores(j, s_ref):
        rows = pl.ds(pl.multiple_of(j * t, t), t)
        for g in range(heads):
            k_aug = jnp.concatenate([k_ref[rows, g * DA_V_DIM:(g + 1) * DA_V_DIM], kb_ref[g]], axis=1)
            for mp in range(2):
                s_ref[2 * g + mp] = _dot_nt(k_aug, q_maps[2 * g + mp])

    def consume(j, s_ref, diagonal=False):
        for g in range(heads):
            vt_aug = jnp.concatenate([vt_ref[j, g * DA_V_DIM:(g + 1) * DA_V_DIM, :], ones_rows], axis=0)
            shift = slope_log2[g] * ((j - i) * t).astype(F32)
            for mp in range(2):
                c = 2 * g + mp
                s = s_ref[c]
                if diagonal:
                    s = s + causal_ref[...]
                m_old = m_ref[c]
                m_new = jnp.maximum(m_old, jnp.max(s, axis=0, keepdims=True) + shift)
                p = jnp.exp2(s - (m_new - shift))
                acc_ref[c] = jnp.exp2(m_old - m_new) * acc_ref[c] + _dot(vt_aug, p.astype(BF16))
                m_ref[c] = m_new

    scores(0, sa_ref)

    def tile_pair(jj, carry):
        j = 2 * jj
        scores(j + 1, sb_ref)
        consume(j, sa_ref)
        scores(j + 2, sa_ref)
        consume(j + 1, sb_ref)
        return carry

    lax.fori_loop(0, i // 2, tile_pair, 0)

    @pl.when(i % 2 == 0)
    def _():
        consume(i, sa_ref, diagonal=True)

    @pl.when(i % 2 == 1)
    def _():
        scores(i, sb_ref)
        consume(i - 1, sa_ref)
        consume(i, sb_ref, diagonal=True)

    lam = (jnp.exp(jnp.sum(lq1_ref[...] * lk1_ref[...], axis=-1, keepdims=True))
           - jnp.exp(jnp.sum(lq2_ref[...] * lk2_ref[...], axis=-1, keepdims=True)) + lam_init)
    for g in range(heads):
        a1 = acc_ref[2 * g]
        a2 = acc_ref[2 * g + 1]
        o_t = (a1[:DA_V_DIM] / a1[DA_V_DIM:DA_V_DIM + 1]
               - lam * (a2[:DA_V_DIM] / a2[DA_V_DIM:DA_V_DIM + 1]))
        o = o_t.T
        o = o * lax.rsqrt(jnp.mean(o * o, axis=-1, keepdims=True) + LN_EPS)
        o_ref[:, g * DA_V_DIM:(g + 1) * DA_V_DIM] = (o * sw_ref[...] * (1.0 - lam_init)).astype(o_ref.dtype)


def _diff_attention(qk, vtb, slopes, lq1, lk1, lq2, lk2, subln_w, batch, seq, lam_init, t, heads):
    n_tok = qk.shape[0]
    nq = seq // t
    width = heads * DA_V_DIM
    key_bias, query_bias = _alibi_tables(t)
    pos = np.arange(t)
    causal = jnp.asarray(np.where(pos[:, None] <= pos[None, :], 0.0, NEG_INF), F32)
    vec = pl.BlockSpec((1, DA_QK_DIM), lambda b, h, i: (0, 0))
    bias_spec = pl.BlockSpec((heads, t, DA_V_DIM), lambda b, h, i: (h, 0, 0))
    maps = 2 * heads
    return pl.pallas_call(
        functools.partial(_attn_kernel, t=t, heads=heads, lam_init=lam_init),
        grid=(batch, DA_HEADS // heads, nq),
        in_specs=[
            pl.BlockSpec(memory_space=pltpu.SMEM),
            vec, vec, vec, vec,
            pl.BlockSpec((1, DA_V_DIM), lambda b, h, i: (0, 0)),
            pl.BlockSpec((t, width), lambda b, h, i: (b * nq + i, h)),
            bias_spec,
            pl.BlockSpec((seq, width), lambda b, h, i: (b, DA_HEADS // heads + h)),
            bias_spec,
            pl.BlockSpec((nq, width, t), lambda b, h, i: (b, h, 0)),
            pl.BlockSpec((t, t), lambda b, h, i: (0, 0)),
        ],
        out_specs=pl.BlockSpec((t, width), lambda b, h, i: (b * nq + i, h)),
        out_shape=jax.ShapeDtypeStruct((n_tok, DA_WIDTH), BF16),
        scratch_shapes=[
            pltpu.VMEM((maps, t, t), F32),
            pltpu.VMEM((maps, t, t), F32),
            pltpu.VMEM((maps, 1, t), F32),
            pltpu.VMEM((maps, DA_V_DIM + ATTN_ONES_ROWS, t), F32),
        ],
        compiler_params=_params("parallel", "parallel", "arbitrary"),
        name="diff_attention",
    )(slopes, lq1, lk1, lq2, lk2, subln_w, qk, query_bias, qk, key_bias, vtb, causal)


def _mix_kernel(x_ref, xh_ref, o_ref, wp_ref, wgp_ref, wga_ref, poolw_ref, pscale_ref, wbp_ref, wba_ref,
                wout_ref, g_ref, b_ref, out_ref, e_ref, *, tm, seq):
    i = pl.program_id(0)
    start = (i * tm) % seq
    x = x_ref[...]
    xb = x.astype(BF16)

    p = _dot(xb, wp_ref[...])
    p_halo = _dot(xh_ref[...].astype(BF16), wp_ref[...])
    e_ref[0:POOL_HALO, :] = jnp.where(start == 0, 0.0, p_halo)
    e_ref[POOL_HALO:, :] = p

    pos = (start + 1 + lax.broadcasted_iota(jnp.int32, (tm, 1), 0)).astype(F32)
    ys = []
    for g, w in enumerate(POOL_WINDOWS):
        cols = slice(g * POOL_GROUP_DIM, (g + 1) * POOL_GROUP_DIM)
        acc = e_ref[POOL_HALO:POOL_HALO + tm, cols]
        for back in range(1, w):
            acc = acc + e_ref[POOL_HALO - back:POOL_HALO - back + tm, cols]
        pooled = acc / jnp.minimum(pos, float(w)) - e_ref[POOL_HALO:POOL_HALO + tm, cols]
        ys.append(_dot(pooled.astype(BF16), poolw_ref[g]))
    y = jnp.concatenate(ys, axis=-1) * pscale_ref[...]
    y_pool = _dot(y.astype(BF16), wbp_ref[...])

    gate_p = _dot(xb, wgp_ref[...])
    gate_a = _dot(xb, wga_ref[...])
    y_attn = _dot(o_ref[...], wba_ref[...])
    merged = jax.nn.sigmoid(gate_p) * y_pool + jax.nn.sigmoid(gate_a) * y_attn
    z = DN_ALPHA * x + _dot(merged.astype(BF16), wout_ref[...])
    out_ref[...] = _layer_norm(z, g_ref[...], b_ref[...])


def _mix_ln1(x2d, o, w_p, w_gp, w_ga, pool_w, pool_scale, w_bp, w_ba, w_out, g, b, seq, tm):
    n_tok = x2d.shape[0]
    halo_blocks = tm // POOL_HALO

    def full(a):
        return pl.BlockSpec(a.shape, lambda i: (0,) * a.ndim)

    return pl.pallas_call(
        functools.partial(_mix_kernel, tm=tm, seq=seq),
        grid=(n_tok // tm,),
        in_specs=[
            pl.BlockSpec((tm, D_MODEL), lambda i: (i, 0)),
            pl.BlockSpec((POOL_HALO, D_MODEL), lambda i: (jnp.maximum(i * halo_blocks - 1, 0), 0)),
            pl.BlockSpec((tm, DA_WIDTH), lambda i: (i, 0)),
            full(w_p), full(w_gp), full(w_ga), full(pool_w), full(pool_scale), full(w_bp), full(w_ba),
            full(w_out), full(g), full(b),
        ],
        out_specs=pl.BlockSpec((tm, D_MODEL), lambda i: (i, 0)),
        out_shape=jax.ShapeDtypeStruct((n_tok, D_MODEL), F32),
        scratch_shapes=[pltpu.VMEM((tm + POOL_HALO, POOL_WIDTH), F32)],
        compiler_params=_params("parallel"),
        name="mix_ln1",
    )(x2d, x2d, o, w_p, w_gp, w_ga, pool_w, pool_scale, w_bp, w_ba, w_out, g, b)


def _xattn_kernel(x_ref, kv_ref, wq_ref, wo_ref, g_ref, b_ref, out_ref):
    x = x_ref[...]
    q = (_dot(x.astype(BF16), wq_ref[...]) * (XA_HEAD_DIM ** -0.5)).astype(BF16)
    kv = kv_ref[0]
    outs = []
    for h in range(XA_HEADS):
        cols = slice(h * XA_HEAD_DIM, (h + 1) * XA_HEAD_DIM)
        s = _dot_nt(q[:, cols], kv[:, cols])
        p = jnp.exp(s - jnp.max(s, axis=-1, keepdims=True))
        l = jnp.sum(p, axis=-1, keepdims=True)
        v = kv[:, D_MODEL + h * XA_HEAD_DIM:D_MODEL + (h + 1) * XA_HEAD_DIM]
        outs.append(_dot(p.astype(BF16), v) / l)
    o = jnp.concatenate(outs, axis=-1)
    z = DN_ALPHA * x + _dot(o.astype(BF16), wo_ref[...])
    out_ref[...] = _layer_norm(z, g_ref[...], b_ref[...])


def _xattn_ln2(x1, kvm, w_cq, w_co, g, b, seq, tm):
    n_tok = x1.shape[0]
    n_mem = kvm.shape[1]

    def full(a):
        return pl.BlockSpec(a.shape, lambda i: (0,) * a.ndim)

    return pl.pallas_call(
        _xattn_kernel,
        grid=(n_tok // tm,),
        in_specs=[
            pl.BlockSpec((tm, D_MODEL), lambda i: (i, 0)),
            pl.BlockSpec((1, n_mem, 2 * D_MODEL), lambda i: ((i * tm) // seq, 0, 0)),
            full(w_cq), full(w_co), full(g), full(b),
        ],
        out_specs=pl.BlockSpec((tm, D_MODEL), lambda i: (i, 0)),
        out_shape=jax.ShapeDtypeStruct((n_tok, D_MODEL), F32),
        compiler_params=_params("parallel"),
        name="xattn_ln2",
    )(x1, kvm, w_cq, w_co, g, b)


SUBLANES = 8


def _sort_network(n):
    def merge(lo, hi, r):
        step = 2 * r
        if step < hi - lo:
            yield from merge(lo, hi, step)
            yield from merge(lo + r, hi, step)
            yield from ((i, i + r) for i in range(lo + r, hi - r, step))
        else:
            yield (lo, lo + r)

    def sort(lo, hi):
        if hi - lo >= 1:
            mid = lo + (hi - lo) // 2
            yield from sort(lo, mid)
            yield from sort(mid + 1, hi)
            yield from merge(lo, hi, 1)

    return tuple(sort(0, n - 1))


def _exchange(v, i, j):
    if v[j] is None:
        return
    if v[i] is None:
        v[i], v[j] = v[j], None
        return
    v[i], v[j] = jnp.maximum(v[i], v[j]), jnp.minimum(v[i], v[j])


def _top_sorted(blocks, k):
    v = list(blocks) + [None] * (k - len(blocks))
    for i, j in _sort_network(k):
        _exchange(v, i, j)
    shift = SUBLANES // 2
    while shift >= 1:
        other = [None if a is None else pltpu.roll(a, shift, 0) for a in v]
        v = [b if a is None else (a if b is None else jnp.maximum(a, b)) for a, b in zip(v, reversed(other))]
        stride = k // 2
        while stride >= 1:
            for i in range(k):
                if not i & stride:
                    _exchange(v, i, i + stride)
            stride //= 2
        shift //= 2
    return v


def _route_kernel(x_ref, wq_ref, keys_ref, rank2_ref, e2_ref, count_ref, d_ref):
    k = PEER_TOPK
    q_t = _dot_nt(wq_ref[...], x_ref[...].astype(BF16)).astype(BF16)
    s1 = _dot(keys_ref[0, 0], q_t[:PEER_HALF])
    s2 = _dot(keys_ref[0, 1], q_t[PEER_HALF:])
    n_blocks = PEER_N_KEYS // SUBLANES
    s1_blocks = [s1[r * SUBLANES:(r + 1) * SUBLANES] for r in range(n_blocks)]
    s2_blocks = [s2[r * SUBLANES:(r + 1) * SUBLANES] for r in range(n_blocks)]
    v1 = _top_sorted(s1_blocks, k)
    v2 = _top_sorted(s2_blocks, k)

    row = lax.broadcasted_iota(jnp.int32, v1[0].shape, 0)

    def pack(vals):
        out = vals[0]
        for r in range(1, SUBLANES):
            out = jnp.where(row == r, vals[r], out)
        return out

    v1_lo, v1_hi, v2_hi = pack(v1[:SUBLANES]), pack(v1[SUBLANES:]), pack(v2[SUBLANES:])
    cand = ([v1_lo + v2[0], v1_hi + v2[0]] + [v1_lo + v2[b] for b in range(1, SUBLANES)] + [v2_hi + v1[0]])
    top = _top_sorted(cand, k)
    tau = top[k - 1]
    z = jnp.ones(tau.shape, F32)
    for c in top[1:]:
        z = z + jnp.exp(c - top[0])
    inv_z = 1.0 / z

    rank2, e2, count, d = [], [], [], []
    for blk1, blk2 in zip(s1_blocks, s2_blocks):
        r2 = jnp.full(blk2.shape, float(k), F32)
        cnt = jnp.zeros(blk1.shape, F32)
        for b in reversed(range(k)):
            r2 = jnp.where(blk2 >= v2[b], float(b), r2)
        for b in range(k):
            cnt = jnp.where(blk1 + v2[b] >= tau, float(b + 1), cnt)
        rank2.append(r2)
        count.append(cnt)
        e2.append(jnp.exp(blk2 - v2[0]))
        d.append(jnp.exp(blk1 - v1[0]) * inv_z)
    rank2_ref[0] = jnp.concatenate(rank2, axis=0).astype(BF16)
    e2_ref[0] = jnp.concatenate(e2, axis=0).astype(BF16)
    count_ref[0] = jnp.concatenate(count, axis=0)
    d_ref[0] = jnp.concatenate(d, axis=0)


def _peer_route(x2, wq_t, keys, tm):
    n_tok = x2.shape[0]
    out_spec = pl.BlockSpec((1, PEER_N_KEYS, tm), lambda i, h: (h, 0, i))
    out_shape = [jax.ShapeDtypeStruct((PEER_HEADS, PEER_N_KEYS, n_tok), dt) for dt in (BF16, BF16, F32, F32)]
    return pl.pallas_call(
        _route_kernel,
        grid=(n_tok // tm, PEER_HEADS),
        in_specs=[
            pl.BlockSpec((tm, D_MODEL), lambda i, h: (i, 0)),
            pl.BlockSpec((PEER_QUERY_DIM, D_MODEL), lambda i, h: (h, 0)),
            pl.BlockSpec((1, 2, PEER_N_KEYS, PEER_HALF), lambda i, h: (h, 0, 0, 0)),
        ],
        out_specs=[out_spec] * 4,
        out_shape=out_shape,
        compiler_params=_params("parallel", "parallel"),
        name="peer_route",
    )(x2, wq_t, keys)


def _peer_kernel(x_ref, u_ref, vt_ref, rank2_ref, e2_ref, count_ref, d_ref, g_ref, b_ref, out_ref,
                 xb_ref, acc_ref, a_ref, *, chunks, rows_per_chunk):
    e = pl.program_id(1)

    @pl.when(e == 0)
    def _():
        xb_ref[...] = x_ref[...].astype(BF16)
        acc_ref[...] = jnp.zeros(acc_ref.shape, F32)

    xb = xb_ref[...]
    chunk = rows_per_chunk * PEER_N_KEYS
    tb = xb.shape[0]

    def sublane_rows(ref, h, row):
        tile = jnp.broadcast_to(ref[h, row:row + 1, :], (16, tb)).astype(BF16)
        return jnp.concatenate([tile] * (PEER_N_KEYS // 16), axis=0)

    def activations(c):
        return _dot_nt(u_ref[c * chunk:(c + 1) * chunk, :], xb)

    a_ref[0] = activations(0)
    for c in range(chunks):
        if c + 1 < chunks:
            a_ref[(c + 1) % 2] = activations(c + 1)
        parts = []
        for r in range(rows_per_chunk):
            row = c * rows_per_chunk + r
            z = a_ref[c % 2, r * PEER_N_KEYS:(r + 1) * PEER_N_KEYS, :]
            act = z + z * lax.erf(z)
            gate = None
            for h in range(PEER_HEADS):
                cnt = sublane_rows(count_ref, h, row)
                d = sublane_rows(d_ref, h, row)
                term = jnp.where(rank2_ref[h] < cnt, e2_ref[h], jnp.zeros((), BF16)) * d
                gate = term if gate is None else gate + term
            parts.append(gate * act.astype(BF16))
        w = jnp.concatenate(parts, axis=0)
        acc_ref[...] += _dot(vt_ref[:, c * chunk:(c + 1) * chunk], w)

    @pl.when(e == pl.num_programs(1) - 1)
    def _():
        z = DN_ALPHA * x_ref[...] + acc_ref[...].T
        out_ref[...] = _layer_norm(z, g_ref[...], b_ref[...])


def _peer_ln3(x2, u, vt, rank2, e2, count, d, g, b, tb, te, chunk):
    n_tok = x2.shape[0]
    rows_per_tile = te // PEER_N_KEYS
    key_spec = pl.BlockSpec((PEER_HEADS, PEER_N_KEYS, tb), lambda t, e: (0, 0, t))
    row_spec = pl.BlockSpec((PEER_HEADS, rows_per_tile, tb), lambda t, e: (0, e, t))
    vec = pl.BlockSpec((1, D_MODEL), lambda t, e: (0, 0))
    return pl.pallas_call(
        functools.partial(_peer_kernel, chunks=te // chunk, rows_per_chunk=chunk // PEER_N_KEYS),
        grid=(n_tok // tb, PEER_N_EXPERTS // te),
        in_specs=[
            pl.BlockSpec((tb, D_MODEL), lambda t, e: (t, 0)),
            pl.BlockSpec((te, D_MODEL), lambda t, e: (e, 0)),
            pl.BlockSpec((D_MODEL, te), lambda t, e: (0, e)),
            key_spec, key_spec, row_spec, row_spec,
            vec, vec,
        ],
        out_specs=pl.BlockSpec((tb, D_MODEL), lambda t, e: (t, 0)),
        out_shape=jax.ShapeDtypeStruct((n_tok, D_MODEL), F32),
        scratch_shapes=[
            pltpu.VMEM((tb, D_MODEL), BF16),
            pltpu.VMEM((D_MODEL, tb), F32),
            pltpu.VMEM((2, chunk, tb), F32),
        ],
        compiler_params=_params("parallel", "arbitrary"),
        name="peer_dense",
    )(x2, u, vt, rank2, e2, count, d, g, b)


def kernel(x, mem, w_in, pool_w, pool_scale, w_br_pool, lambda_q1, lambda_k1, lambda_q2, lambda_k2, subln_w, w_br_attn, w_out, ln1_g, ln1_b, w_cq, w_ckv, w_co, ln2_g, ln2_b, w_pq, sub_keys, expert_u, expert_v, ln3_g, ln3_b):
    batch, seq, _ = x.shape
    n_tok = batch * seq
    n_mem = mem.shape[1]
    attn_tile = 512
    slopes = jnp.exp2(-8.0 * jnp.arange(1, DA_HEADS + 1, dtype=F32) / DA_HEADS)

    def row(v):
        return v.reshape(1, -1).astype(F32)

    for l in range(DEPTH):
        lam_init = 0.8 - 0.6 * math.exp(-0.3 * l)
        x2d = x.reshape(n_tok, D_MODEL)
        w_in_b = w_in[l].astype(BF16)
        q0 = POOL_WIDTH
        k0 = q0 + DA_WIDTH
        v0 = k0 + DA_WIDTH
        gp0 = v0 + DA_WIDTH
        ga0 = gp0 + D_MODEL

        qk_scale = jnp.concatenate([jnp.full((1, DA_WIDTH), LOG2E * DA_QK_DIM ** -0.5, F32),
                                    jnp.ones((1, DA_WIDTH), F32)], axis=1)
        qk = _proj(x2d, w_in_b, q0 // 512, qk_scale, BF16, 1024, 512, "proj_qk")
        vtb = _proj_t_blocked(x2d, w_in_b[:, v0:gp0].T, attn_tile, "proj_vt")
        o = _diff_attention(qk, vtb, slopes, row(lambda_q1[l]), row(lambda_k1[l]), row(lambda_q2[l]),
                            row(lambda_k2[l]), row(subln_w[l]), batch, seq, lam_init, attn_tile, 2)

        x1 = _mix_ln1(x2d, o, w_in_b[:, :POOL_WIDTH], w_in_b[:, gp0:ga0], w_in_b[:, ga0:],
                      pool_w[l].astype(BF16), row(pool_scale[l]), w_br_pool[l].astype(BF16),
                      w_br_attn[l].astype(BF16), w_out[l].astype(BF16), row(ln1_g[l]), row(ln1_b[l]), seq, 512)

        kvm = _proj(mem.reshape(batch * n_mem, D_MODEL), w_ckv[l].astype(BF16), 0, jnp.ones((1, 2 * D_MODEL), F32),
                    BF16, batch * n_mem, 512, "proj_mem_kv").reshape(batch, n_mem, 2 * D_MODEL)
        x2 = _xattn_ln2(x1, kvm, w_cq[l].astype(BF16), w_co[l].astype(BF16), row(ln2_g[l]), row(ln2_b[l]), seq, 512)

        rank2, e2, count, d = _peer_route(x2, w_pq[l].T.astype(BF16), sub_keys[l].astype(BF16), 512)
        x3 = _peer_ln3(x2, (expert_u[l] * SQRT_HALF).astype(BF16), (expert_v[l].T * SQRT_HALF).astype(BF16),
                       rank2, e2, count, d,
                       row(ln3_g[l]), row(ln3_b[l]), 512, 2048, 1024)
        x = x3.reshape(batch, seq, D_MODEL)
    return x
```

```python
import functools
import math

import jax
import jax.numpy as jnp
import numpy as np
from jax import lax
from jax.experimental import pallas as pl
from jax.experimental.pallas import tpu as pltpu

F32 = jnp.float32
BF16 = jnp.bfloat16

D_MODEL = 1024
POOL_WINDOWS = (2, 4, 8, 16)
POOL_GROUPS = len(POOL_WINDOWS)
POOL_WIDTH = D_MODEL // 2
POOL_GROUP_DIM = POOL_WIDTH // POOL_GROUPS
POOL_HALO = max(POOL_WINDOWS)
DA_HEADS = 8
DA_QK_DIM = 64
DA_V_DIM = 2 * DA_QK_DIM
DA_WIDTH = DA_HEADS * DA_V_DIM
XA_HEADS = 4
XA_HEAD_DIM = D_MODEL // XA_HEADS
PEER_HEADS = 8
PEER_N_KEYS = 128
PEER_N_EXPERTS = PEER_N_KEYS * PEER_N_KEYS
PEER_QUERY_DIM = 128
PEER_HALF = PEER_QUERY_DIM // 2
PEER_TOPK = 16
DEPTH = 1
DN_ALPHA = (2 * DEPTH) ** 0.25
LN_EPS = 1e-5
NEG_INF = -1e30
SQRT_HALF = math.sqrt(0.5)
LOG2E = float(np.float32(math.log2(math.e)))

VMEM_LIMIT_BYTES = 52 * 1024 * 1024

_NT = (((1,), (1,)), ((), ()))


def _params(*sem):
    return pltpu.CompilerParams(dimension_semantics=sem, vmem_limit_bytes=VMEM_LIMIT_BYTES)


def _dot(a, b):
    return jnp.dot(a, b, preferred_element_type=F32)


def _dot_nt(a, b):
    return lax.dot_general(a, b, _NT, preferred_element_type=F32)


def _layer_norm(z, g, b):
    mu = jnp.mean(z, axis=-1, keepdims=True)
    zc = z - mu
    var = jnp.mean(zc * zc, axis=-1, keepdims=True)
    return zc * lax.rsqrt(var + LN_EPS) * g + b


def _proj_kernel(a_ref, w_ref, scale_ref, o_ref):
    o_ref[...] = (_dot(a_ref[...].astype(BF16), w_ref[...]) * scale_ref[...]).astype(o_ref.dtype)


def _proj(a, w, col_block0, col_scale, out_dtype, tm, tn, name):
    m, k = a.shape
    n_cols = col_scale.shape[1]
    return pl.pallas_call(
        _proj_kernel,
        grid=(m // tm, n_cols // tn),
        in_specs=[
            pl.BlockSpec((tm, k), lambda i, j: (i, 0)),
            pl.BlockSpec((k, tn), lambda i, j: (0, j + col_block0)),
            pl.BlockSpec((1, tn), lambda i, j: (0, j)),
        ],
        out_specs=pl.BlockSpec((tm, tn), lambda i, j: (i, j)),
        out_shape=jax.ShapeDtypeStruct((m, n_cols), out_dtype),
        compiler_params=_params("parallel", "parallel"),
        name=name,
    )(a, w, col_scale)


def _proj_qkv_kernel(x_ref, wqk_ref, scale_ref, wvt_ref, qk_ref, vt_ref):
    xb = x_ref[...].astype(BF16)
    qk_ref[...] = (_dot(xb, wqk_ref[...]) * scale_ref[...]).astype(qk_ref.dtype)
    vt_ref[0] = _dot_nt(wvt_ref[...], xb).astype(vt_ref.dtype)


def _proj_qkv(x2d, w_qk, qk_scale, w_v_t, tk):
    m, k = x2d.shape
    n = w_qk.shape[1]
    nv = w_v_t.shape[0]
    return pl.pallas_call(
        _proj_qkv_kernel,
        grid=(m // tk,),
        in_specs=[
            pl.BlockSpec((tk, k), lambda j: (j, 0)),
            pl.BlockSpec((k, n), lambda j: (0, 0)),
            pl.BlockSpec((1, n), lambda j: (0, 0)),
            pl.BlockSpec((nv, k), lambda j: (0, 0)),
        ],
        out_specs=[pl.BlockSpec((tk, n), lambda j: (j, 0)), pl.BlockSpec((1, nv, tk), lambda j: (j, 0, 0))],
        out_shape=[jax.ShapeDtypeStruct((m, n), BF16), jax.ShapeDtypeStruct((m // tk, nv, tk), BF16)],
        compiler_params=_params("parallel"),
        name="proj_qkv",
    )(x2d, w_qk, qk_scale, w_v_t)


ATTN_ONES_ROWS = 16
ALIBI_PIECES = 5


def _alibi_tables(t):
    slopes = np.exp2(-8.0 * np.arange(1, DA_HEADS + 1, dtype=np.float64) / DA_HEADS)
    ramp = float(LOG2E) * slopes[:, None] * np.arange(t, dtype=np.float64)[None, :]

    def pieces(v):
        out = []
        for _ in range(ALIBI_PIECES):
            piece = v.astype(np.float32).astype(BF16).astype(np.float64)
            out.append(piece)
            v = v - piece
        assert not v.any()
        return out

    ones = np.ones_like(ramp)
    pad = [np.zeros_like(ramp)] * (DA_V_DIM - 2 * ALIBI_PIECES)
    key_side = np.stack(pieces(ramp) + [ones] * ALIBI_PIECES + pad, axis=-1)
    query_side = np.stack([ones] * ALIBI_PIECES + pieces(-ramp) + pad, axis=-1)
    return jnp.asarray(key_side, BF16), jnp.asarray(query_side, BF16)


def _attn_kernel(slopes_ref, lq1_ref, lk1_ref, lq2_ref, lk2_ref, sw_ref, q_ref, qb_ref, k_ref, kb_ref, vt_ref,
                 causal_ref, o_ref, sa_ref, sb_ref, m_ref, acc_ref, *, t, heads, lam_init):
    hg = pl.program_id(1)
    i = pl.program_id(2)
    slope_log2 = [slopes_ref[hg * heads + g] * LOG2E for g in range(heads)]

    lane = lax.broadcasted_iota(jnp.int32, (t, DA_V_DIM), 1)
    q_maps = []
    for g in range(heads):
        q = q_ref[:, g * DA_V_DIM:(g + 1) * DA_V_DIM]
        zero = jnp.zeros_like(q)
        q_maps += [jnp.concatenate([jnp.where(lane < DA_QK_DIM, q, zero), qb_ref[g]], axis=1),
                   jnp.concatenate([jnp.where(lane >= DA_QK_DIM, q, zero), qb_ref[g]], axis=1)]
    ones_rows = jnp.ones((ATTN_ONES_ROWS, t), BF16)

    m_ref[...] = jnp.full(m_ref.shape, NEG_INF, F32)
    acc_ref[...] = jnp.zeros(acc_ref.shape, F32)

    def scores(j, s_ref):
        rows = pl.ds(pl.multiple_of(j * t, t), t)
        for g in range(heads):
            k_aug = jnp.concatenate([k_ref[rows, g * DA_V_DIM:(g + 1) * DA_V_DIM], kb_ref[g]], axis=1)
            for mp in range(2):
                s_ref[2 * g + mp] = _dot_nt(k_aug, q_maps[2 * g + mp])

    def consume(j, s_ref, diagonal=False):
        for g in range(heads):
            vt_aug = jnp.concatenate([vt_ref[j, g * DA_V_DIM:(g + 1) * DA_V_DIM, :], ones_rows], axis=0)
            shift = slope_log2[g] * ((j - i) * t).astype(F32)
            for mp in range(2):
                c = 2 * g + mp
                s = s_ref[c]
                if diagonal:
                    s = s + causal_ref[...]
                m_old = m_ref[c]
                m_new = jnp.maximum(m_old, jnp.max(s, axis=0, keepdims=True) + shift)
                p = jnp.exp2(s - (m_new - shift))
                acc_ref[c] = jnp.exp2(m_old - m_new) * acc_ref[c] + _dot(vt_aug, p.astype(BF16))
                m_ref[c] = m_new

    scores(0, sa_ref)

    def tile_pair(jj, carry):
        j = 2 * jj
        scores(j + 1, sb_ref)
        consume(j, sa_ref)
        scores(j + 2, sa_ref)
        consume(j + 1, sb_ref)
        return carry

    lax.fori_loop(0, i // 2, tile_pair, 0)

    @pl.when(i % 2 == 0)
    def _():
        consume(i, sa_ref, diagonal=True)

    @pl.when(i % 2 == 1)
    def _():
        scores(i, sb_ref)
        consume(i - 1, sa_ref)
        consume(i, sb_ref, diagonal=True)

    lam = (jnp.exp(jnp.sum(lq1_ref[...] * lk1_ref[...], axis=-1, keepdims=True))
           - jnp.exp(jnp.sum(lq2_ref[...] * lk2_ref[...], axis=-1, keepdims=True)) + lam_init)
    for g in range(heads):
        a1 = acc_ref[2 * g]
        a2 = acc_ref[2 * g + 1]
        o_t = (a1[:DA_V_DIM] / a1[DA_V_DIM:DA_V_DIM + 1]
               - lam * (a2[:DA_V_DIM] / a2[DA_V_DIM:DA_V_DIM + 1]))
        o = o_t.T
        o = o * lax.rsqrt(jnp.mean(o * o, axis=-1, keepdims=True) + LN_EPS)
        o_ref[:, g * DA_V_DIM:(g + 1) * DA_V_DIM] = (o * sw_ref[...] * (1.0 - lam_init)).astype(o_ref.dtype)


def _diff_attention(qk, vtb, slopes, lq1, lk1, lq2, lk2, subln_w, batch, seq, lam_init, t, heads):
    n_tok = qk.shape[0]
    nq = seq // t
    width = heads * DA_V_DIM
    key_bias, query_bias = _alibi_tables(t)
    pos = np.arange(t)
    causal = jnp.asarray(np.where(pos[:, None] <= pos[None, :], 0.0, NEG_INF), F32)
    vec = pl.BlockSpec((1, DA_QK_DIM), lambda b, h, i: (0, 0))
    bias_spec = pl.BlockSpec((heads, t, DA_V_DIM), lambda b, h, i: (h, 0, 0))
    maps = 2 * heads
    return pl.pallas_call(
        functools.partial(_attn_kernel, t=t, heads=heads, lam_init=lam_init),
        grid=(batch, DA_HEADS // heads, nq),
        in_specs=[
            pl.BlockSpec(memory_space=pltpu.SMEM),
            vec, vec, vec, vec,
            pl.BlockSpec((1, DA_V_DIM), lambda b, h, i: (0, 0)),
            pl.BlockSpec((t, width), lambda b, h, i: (b * nq + i, h)),
            bias_spec,
            pl.BlockSpec((seq, width), lambda b, h, i: (b, DA_HEADS // heads + h)),
            bias_spec,
            pl.BlockSpec((nq, width, t), lambda b, h, i: (b, h, 0)),
            pl.BlockSpec((t, t), lambda b, h, i: (0, 0)),
        ],
        out_specs=pl.BlockSpec((t, width), lambda b, h, i: (b * nq + i, h)),
        out_shape=jax.ShapeDtypeStruct((n_tok, DA_WIDTH), BF16),
        scratch_shapes=[
            pltpu.VMEM((maps, t, t), F32),
            pltpu.VMEM((maps, t, t), F32),
            pltpu.VMEM((maps, 1, t), F32),
            pltpu.VMEM((maps, DA_V_DIM + ATTN_ONES_ROWS, t), F32),
        ],
        compiler_params=_params("parallel", "parallel", "arbitrary"),
        name="diff_attention",
    )(slopes, lq1, lk1, lq2, lk2, subln_w, qk, query_bias, qk, key_bias, vtb, causal)


def _mix_kernel(x_ref, xh_ref, o_ref, wp_ref, wgp_ref, wga_ref, poolw_ref, pscale_ref, wbp_ref, wba_ref,
                wout_ref, g_ref, b_ref, out_ref, e_ref, *, tm, seq):
    i = pl.program_id(0)
    start = (i * tm) % seq
    x = x_ref[...]
    xb = x.astype(BF16)

    p = _dot(xb, wp_ref[...])
    p_halo = _dot(xh_ref[...].astype(BF16), wp_ref[...])
    e_ref[0:POOL_HALO, :] = jnp.where(start == 0, 0.0, p_halo)
    e_ref[POOL_HALO:, :] = p

    pos = (start + 1 + lax.broadcasted_iota(jnp.int32, (tm, 1), 0)).astype(F32)
    ys = []
    for g, w in enumerate(POOL_WINDOWS):
        cols = slice(g * POOL_GROUP_DIM, (g + 1) * POOL_GROUP_DIM)
        acc = e_ref[POOL_HALO:POOL_HALO + tm, cols]
        for back in range(1, w):
            acc = acc + e_ref[POOL_HALO - back:POOL_HALO - back + tm, cols]
        pooled = acc / jnp.minimum(pos, float(w)) - e_ref[POOL_HALO:POOL_HALO + tm, cols]
        ys.append(_dot(pooled.astype(BF16), poolw_ref[g]))
    y = jnp.concatenate(ys, axis=-1) * pscale_ref[...]
    y_pool = _dot(y.astype(BF16), wbp_ref[...])

    gate_p = _dot(xb, wgp_ref[...])
    gate_a = _dot(xb, wga_ref[...])
    y_attn = _dot(o_ref[...], wba_ref[...])
    merged = jax.nn.sigmoid(gate_p) * y_pool + jax.nn.sigmoid(gate_a) * y_attn
    z = DN_ALPHA * x + _dot(merged.astype(BF16), wout_ref[...])
    out_ref[...] = _layer_norm(z, g_ref[...], b_ref[...])


def _mix_ln1(x2d, o, w_p, w_gp, w_ga, pool_w, pool_scale, w_bp, w_ba, w_out, g, b, seq, tm):
    n_tok = x2d.shape[0]
    halo_blocks = tm // POOL_HALO

    def full(a):
        return pl.BlockSpec(a.shape, lambda i: (0,) * a.ndim)

    return pl.pallas_call(
        functools.partial(_mix_kernel, tm=tm, seq=seq),
        grid=(n_tok // tm,),
        in_specs=[
            pl.BlockSpec((tm, D_MODEL), lambda i: (i, 0)),
            pl.BlockSpec((POOL_HALO, D_MODEL), lambda i: (jnp.maximum(i * halo_blocks - 1, 0), 0)),
            pl.BlockSpec((tm, DA_WIDTH), lambda i: (i, 0)),
            full(w_p), full(w_gp), full(w_ga), full(pool_w), full(pool_scale), full(w_bp), full(w_ba),
            full(w_out), full(g), full(b),
        ],
        out_specs=pl.BlockSpec((tm, D_MODEL), lambda i: (i, 0)),
        out_shape=jax.ShapeDtypeStruct((n_tok, D_MODEL), F32),
        scratch_shapes=[pltpu.VMEM((tm + POOL_HALO, POOL_WIDTH), F32)],
        compiler_params=_params("parallel"),
        name="mix_ln1",
    )(x2d, x2d, o, w_p, w_gp, w_ga, pool_w, pool_scale, w_bp, w_ba, w_out, g, b)


def _xattn_kernel(x_ref, kv_ref, wq_ref, wo_ref, g_ref, b_ref, out_ref):
    x = x_ref[...]
    q = (_dot(x.astype(BF16), wq_ref[...]) * (XA_HEAD_DIM ** -0.5)).astype(BF16)
    kv = kv_ref[0]
    outs = []
    for h in range(XA_HEADS):
        cols = slice(h * XA_HEAD_DIM, (h + 1) * XA_HEAD_DIM)
        s = _dot_nt(q[:, cols], kv[:, cols])
        p = jnp.exp(s - jnp.max(s, axis=-1, keepdims=True))
        l = jnp.sum(p, axis=-1, keepdims=True)
        v = kv[:, D_MODEL + h * XA_HEAD_DIM:D_MODEL + (h + 1) * XA_HEAD_DIM]
        outs.append(_dot(p.astype(BF16), v) / l)
    o = jnp.concatenate(outs, axis=-1)
    z = DN_ALPHA * x + _dot(o.astype(BF16), wo_ref[...])
    out_ref[...] = _layer_norm(z, g_ref[...], b_ref[...])


def _xattn_ln2(x1, kvm, w_cq, w_co, g, b, seq, tm):
    n_tok = x1.shape[0]
    n_mem = kvm.shape[1]

    def full(a):
        return pl.BlockSpec(a.shape, lambda i: (0,) * a.ndim)

    return pl.pallas_call(
        _xattn_kernel,
        grid=(n_tok // tm,),
        in_specs=[
            pl.BlockSpec((tm, D_MODEL), lambda i: (i, 0)),
            pl.BlockSpec((1, n_mem, 2 * D_MODEL), lambda i: ((i * tm) // seq, 0, 0)),
            full(w_cq), full(w_co), full(g), full(b),
        ],
        out_specs=pl.BlockSpec((tm, D_MODEL), lambda i: (i, 0)),
        out_shape=jax.ShapeDtypeStruct((n_tok, D_MODEL), F32),
        compiler_params=_params("parallel"),
        name="xattn_ln2",
    )(x1, kvm, w_cq, w_co, g, b)


SUBLANES = 8


def _sort_network(n):
    def merge(lo, hi, r):
        step = 2 * r
        if step < hi - lo:
            yield from merge(lo, hi, step)
            yield from merge(lo + r, hi, step)
            yield from ((i, i + r) for i in range(lo + r, hi - r, step))
        else:
            yield (lo, lo + r)

    def sort(lo, hi):
        if hi - lo >= 1:
            mid = lo + (hi - lo) // 2
            yield from sort(lo, mid)
            yield from sort(mid + 1, hi)
            yield from merge(lo, hi, 1)

    return tuple(sort(0, n - 1))


def _exchange(v, i, j):
    if v[j] is None:
        return
    if v[i] is None:
        v[i], v[j] = v[j], None
        return
    v[i], v[j] = jnp.maximum(v[i], v[j]), jnp.minimum(v[i], v[j])


def _top_sorted(blocks, k):
    v = list(blocks) + [None] * (k - len(blocks))
    for i, j in _sort_network(k):
        _exchange(v, i, j)
    shift = SUBLANES // 2
    while shift >= 1:
        other = [None if a is None else pltpu.roll(a, shift, 0) for a in v]
        v = [b if a is None else (a if b is None else jnp.maximum(a, b)) for a, b in zip(v, reversed(other))]
        stride = k // 2
        while stride >= 1:
            for i in range(k):
                if not i & stride:
                    _exchange(v, i, i + stride)
            stride //= 2
        shift //= 2
    return v


def _route_kernel(x_ref, wq_ref, keys_ref, rank2_ref, e2_ref, count_ref, d_ref, *, heads):
    q_t = _dot_nt(wq_ref[...], x_ref[...].astype(BF16)).astype(BF16)
    for g in range(heads):
        q_g = q_t[g * PEER_QUERY_DIM:(g + 1) * PEER_QUERY_DIM]
        s1 = _dot(keys_ref[g, 0], q_g[:PEER_HALF])
        s2 = _dot(keys_ref[g, 1], q_g[PEER_HALF:])
        _route_head(s1, s2, g, rank2_ref, e2_ref, count_ref, d_ref)


def _route_head(s1, s2, g, rank2_ref, e2_ref, count_ref, d_ref):
    k = PEER_TOPK
    n_blocks = PEER_N_KEYS // SUBLANES
    s1_blocks = [s1[r * SUBLANES:(r + 1) * SUBLANES] for r in range(n_blocks)]
    s2_blocks = [s2[r * SUBLANES:(r + 1) * SUBLANES] for r in range(n_blocks)]
    v1 = _top_sorted(s1_blocks, k)
    v2 = _top_sorted(s2_blocks, k)

    row = lax.broadcasted_iota(jnp.int32, v1[0].shape, 0)

    def pack(vals):
        out = vals[0]
        for r in range(1, SUBLANES):
            out = jnp.where(row == r, vals[r], out)
        return out

    v1_lo, v1_hi, v2_hi = pack(v1[:SUBLANES]), pack(v1[SUBLANES:]), pack(v2[SUBLANES:])
    cand = ([v1_lo + v2[0], v1_hi + v2[0]] + [v1_lo + v2[b] for b in range(1, SUBLANES)] + [v2_hi + v1[0]])
    top = _top_sorted(cand, k)
    tau = top[k - 1]
    z = jnp.ones(tau.shape, F32)
    for c in top[1:]:
        z = z + jnp.exp(c - top[0])
    inv_z = 1.0 / z

    partners = []
    for a in range(k):
        n = jnp.zeros(tau.shape, F32)
        for b in range(k // (a + 1)):
            n = jnp.where(v1[a] + v2[b] >= tau, float(b + 1), n)
        partners.append(n)

    rank2, e2, count, d = [], [], [], []
    for blk1, blk2 in zip(s1_blocks, s2_blocks):
        r2 = jnp.full(blk2.shape, float(k), F32)
        cnt = jnp.zeros(blk1.shape, F32)
        for b in reversed(range(k)):
            r2 = jnp.where(blk2 >= v2[b], float(b), r2)
            cnt = jnp.where(blk1 >= v1[b], partners[b], cnt)
        rank2.append(r2)
        count.append(cnt)
        e2.append(jnp.exp(blk2 - v2[0]))
        d.append(jnp.exp(blk1 - v1[0]) * inv_z)
    rank2_ref[g] = jnp.concatenate(rank2, axis=0).astype(BF16)
    e2_ref[g] = jnp.concatenate(e2, axis=0).astype(BF16)
    count_ref[g] = jnp.concatenate(count, axis=0)
    d_ref[g] = jnp.concatenate(d, axis=0)


def _peer_route(x2, wq_t, keys, tm, heads):
    n_tok = x2.shape[0]
    out_spec = pl.BlockSpec((heads, PEER_N_KEYS, tm), lambda i, h: (h, 0, i))
    out_shape = [jax.ShapeDtypeStruct((PEER_HEADS, PEER_N_KEYS, n_tok), dt) for dt in (BF16, BF16, F32, F32)]
    return pl.pallas_call(
        functools.partial(_route_kernel, heads=heads),
        grid=(n_tok // tm, PEER_HEADS // heads),
        in_specs=[
            pl.BlockSpec((tm, D_MODEL), lambda i, h: (i, 0)),
            pl.BlockSpec((heads * PEER_QUERY_DIM, D_MODEL), lambda i, h: (h, 0)),
            pl.BlockSpec((heads, 2, PEER_N_KEYS, PEER_HALF), lambda i, h: (h, 0, 0, 0)),
        ],
        out_specs=[out_spec] * 4,
        out_shape=out_shape,
        compiler_params=_params("parallel", "parallel"),
        name="peer_route",
    )(x2, wq_t, keys)


def _peer_kernel(x_ref, u_ref, vt_ref, rank2_ref, e2_ref, count_ref, d_ref, g_ref, b_ref, out_ref,
                 xb_ref, acc_ref, a_ref, *, chunks, rows_per_chunk):
    e = pl.program_id(1)

    @pl.when(e == 0)
    def _():
        xb_ref[...] = (x_ref[...] * SQRT_HALF).astype(BF16)
        acc_ref[...] = jnp.zeros(acc_ref.shape, F32)

    xb = xb_ref[...]
    chunk = rows_per_chunk * PEER_N_KEYS
    tb = xb.shape[0]

    def sublane_rows(ref, h, row):
        tile = jnp.broadcast_to(ref[h, row:row + 1, :], (16, tb)).astype(BF16)
        return jnp.concatenate([tile] * (PEER_N_KEYS // 16), axis=0)

    def activations(c):
        return _dot_nt(u_ref[c * chunk:(c + 1) * chunk, :], xb)

    a_ref[0] = activations(0)
    for c in range(chunks):
        if c + 1 < chunks:
            a_ref[(c + 1) % 2] = activations(c + 1)
        parts = []
        for r in range(rows_per_chunk):
            row = c * rows_per_chunk + r
            z = a_ref[c % 2, r * PEER_N_KEYS:(r + 1) * PEER_N_KEYS, :]
            act = z + z * lax.erf(z)
            gate = None
            for h in range(PEER_HEADS):
                cnt = sublane_rows(count_ref, h, row)
                d = sublane_rows(d_ref, h, row)
                term = jnp.where(rank2_ref[h] < cnt, e2_ref[h], jnp.zeros((), BF16)) * d
                gate = term if gate is None else gate + term
            parts.append(gate * act.astype(BF16))
        w = jnp.concatenate(parts, axis=0)
        acc_ref[...] += _dot(vt_ref[:, c * chunk:(c + 1) * chunk], w)

    @pl.when(e == pl.num_programs(1) - 1)
    def _():
        z = DN_ALPHA * x_ref[...] + SQRT_HALF * acc_ref[...].T
        out_ref[...] = _layer_norm(z, g_ref[...], b_ref[...])


def _peer_ln3(x2, u, vt, rank2, e2, count, d, g, b, tb, te, chunk):
    n_tok = x2.shape[0]
    rows_per_tile = te // PEER_N_KEYS
    key_spec = pl.BlockSpec((PEER_HEADS, PEER_N_KEYS, tb), lambda t, e: (0, 0, t))
    row_spec = pl.BlockSpec((PEER_HEADS, rows_per_tile, tb), lambda t, e: (0, e, t))
    vec = pl.BlockSpec((1, D_MODEL), lambda t, e: (0, 0))
    return pl.pallas_call(
        functools.partial(_peer_kernel, chunks=te // chunk, rows_per_chunk=chunk // PEER_N_KEYS),
        grid=(n_tok // tb, PEER_N_EXPERTS // te),
        in_specs=[
            pl.BlockSpec((tb, D_MODEL), lambda t, e: (t, 0)),
            pl.BlockSpec((te, D_MODEL), lambda t, e: (e, 0)),
            pl.BlockSpec((D_MODEL, te), lambda t, e: (0, e)),
            key_spec, key_spec, row_spec, row_spec,
            vec, vec,
        ],
        out_specs=pl.BlockSpec((tb, D_MODEL), lambda t, e: (t, 0)),
        out_shape=jax.ShapeDtypeStruct((n_tok, D_MODEL), F32),
        scratch_shapes=[
            pltpu.VMEM((tb, D_MODEL), BF16),
            pltpu.VMEM((D_MODEL, tb), F32),
            pltpu.VMEM((2, chunk, tb), F32),
        ],
        compiler_params=_params("parallel", "arbitrary"),
        name="peer_dense",
    )(x2, u, vt, rank2, e2, count, d, g, b)


def kernel(x, mem, w_in, pool_w, pool_scale, w_br_pool, lambda_q1, lambda_k1, lambda_q2, lambda_k2, subln_w, w_br_attn, w_out, ln1_g, ln1_b, w_cq, w_ckv, w_co, ln2_g, ln2_b, w_pq, sub_keys, expert_u, expert_v, ln3_g, ln3_b):
    batch, seq, _ = x.shape
    n_tok = batch * seq
    n_mem = mem.shape[1]
    attn_tile = 512
    slopes = jnp.exp2(-8.0 * jnp.arange(1, DA_HEADS + 1, dtype=F32) / DA_HEADS)

    def row(v):
        return v.reshape(1, -1).astype(F32)

    for l in range(DEPTH):
        lam_init = 0.8 - 0.6 * math.exp(-0.3 * l)
        x2d = x.reshape(n_tok, D_MODEL)
        w_in_b = w_in[l].astype(BF16)
        q0 = POOL_WIDTH
        k0 = q0 + DA_WIDTH
        v0 = k0 + DA_WIDTH
        gp0 = v0 + DA_WIDTH
        ga0 = gp0 + D_MODEL

        qk_scale = jnp.concatenate([jnp.full((1, DA_WIDTH), LOG2E * DA_QK_DIM ** -0.5, F32),
                                    jnp.ones((1, DA_WIDTH), F32)], axis=1)
        qk, vtb = _proj_qkv(x2d, w_in_b[:, q0:v0], qk_scale, w_in_b[:, v0:gp0].T, attn_tile)
        o = _diff_attention(qk, vtb, slopes, row(lambda_q1[l]), row(lambda_k1[l]), row(lambda_q2[l]),
                            row(lambda_k2[l]), row(subln_w[l]), batch, seq, lam_init, attn_tile, 2)

        x1 = _mix_ln1(x2d, o, w_in_b[:, :POOL_WIDTH], w_in_b[:, gp0:ga0], w_in_b[:, ga0:],
                      pool_w[l].astype(BF16), row(pool_scale[l]), w_br_pool[l].astype(BF16),
                      w_br_attn[l].astype(BF16), w_out[l].astype(BF16), row(ln1_g[l]), row(ln1_b[l]), seq, 512)

        kvm = _proj(mem.reshape(batch * n_mem, D_MODEL), w_ckv[l].astype(BF16), 0, jnp.ones((1, 2 * D_MODEL), F32),
                    BF16, batch * n_mem, 512, "proj_mem_kv").reshape(batch, n_mem, 2 * D_MODEL)
        x2 = _xattn_ln2(x1, kvm, w_cq[l].astype(BF16), w_co[l].astype(BF16), row(ln2_g[l]), row(ln2_b[l]), seq, 512)

        rank2, e2, count, d = _peer_route(x2, w_pq[l].T.astype(BF16), sub_keys[l].astype(BF16), 512, 4)
        x3 = _peer_ln3(x2, expert_u[l].astype(BF16), expert_v[l].T.astype(BF16), rank2, e2, count, d,
                       row(ln3_g[l]), row(ln3_b[l]), 512, 2048, 1024)
        x = x3.reshape(batch, seq, D_MODEL)
    return x
```

```python
import functools
import math

import jax
import jax.numpy as jnp
import numpy as np
from jax import lax
from jax.experimental import pallas as pl
from jax.experimental.pallas import tpu as pltpu

F32 = jnp.float32
BF16 = jnp.bfloat16

D_MODEL = 1024
POOL_WINDOWS = (2, 4, 8, 16)
POOL_GROUPS = len(POOL_WINDOWS)
POOL_WIDTH = D_MODEL // 2
POOL_GROUP_DIM = POOL_WIDTH // POOL_GROUPS
POOL_HALO = max(POOL_WINDOWS)
DA_HEADS = 8
DA_QK_DIM = 64
DA_V_DIM = 2 * DA_QK_DIM
DA_WIDTH = DA_HEADS * DA_V_DIM
XA_HEADS = 4
XA_HEAD_DIM = D_MODEL // XA_HEADS
PEER_HEADS = 8
PEER_N_KEYS = 128
PEER_N_EXPERTS = PEER_N_KEYS * PEER_N_KEYS
PEER_QUERY_DIM = 128
PEER_HALF = PEER_QUERY_DIM // 2
PEER_TOPK = 16
DEPTH = 1
DN_ALPHA = (2 * DEPTH) ** 0.25
LN_EPS = 1e-5
NEG_INF = -1e30
SQRT_HALF = math.sqrt(0.5)
LOG2E = float(np.float32(math.log2(math.e)))

VMEM_LIMIT_BYTES = 52 * 1024 * 1024

_NT = (((1,), (1,)), ((), ()))


def _params(*sem):
    return pltpu.CompilerParams(dimension_semantics=sem, vmem_limit_bytes=VMEM_LIMIT_BYTES)


def _dot(a, b):
    return jnp.dot(a, b, preferred_element_type=F32)


def _dot_nt(a, b):
    return lax.dot_general(a, b, _NT, preferred_element_type=F32)


def _layer_norm(z, g, b):
    mu = jnp.mean(z, axis=-1, keepdims=True)
    zc = z - mu
    var = jnp.mean(zc * zc, axis=-1, keepdims=True)
    return zc * lax.rsqrt(var + LN_EPS) * g + b


def _proj_kernel(a_ref, w_ref, scale_ref, o_ref):
    o_ref[...] = (_dot(a_ref[...].astype(BF16), w_ref[...]) * scale_ref[...]).astype(o_ref.dtype)


def _proj(a, w, col_block0, col_scale, out_dtype, tm, tn, name):
    m, k = a.shape
    n_cols = col_scale.shape[1]
    return pl.pallas_call(
        _proj_kernel,
        grid=(m // tm, n_cols // tn),
        in_specs=[
            pl.BlockSpec((tm, k), lambda i, j: (i, 0)),
            pl.BlockSpec((k, tn), lambda i, j: (0, j + col_block0)),
            pl.BlockSpec((1, tn), lambda i, j: (0, j)),
        ],
        out_specs=pl.BlockSpec((tm, tn), lambda i, j: (i, j)),
        out_shape=jax.ShapeDtypeStruct((m, n_cols), out_dtype),
        compiler_params=_params("parallel", "parallel"),
        name=name,
    )(a, w, col_scale)


def _proj_qkv_kernel(x_ref, wqk_ref, scale_ref, wvt_ref, qk_ref, vt_ref):
    xb = x_ref[...].astype(BF16)
    qk_ref[...] = (_dot(xb, wqk_ref[...]) * scale_ref[...]).astype(qk_ref.dtype)
    vt_ref[0] = _dot_nt(wvt_ref[...], xb).astype(vt_ref.dtype)


def _proj_qkv(x2d, w_qk, qk_scale, w_v_t, tk):
    m, k = x2d.shape
    n = w_qk.shape[1]
    nv = w_v_t.shape[0]
    return pl.pallas_call(
        _proj_qkv_kernel,
        grid=(m // tk,),
        in_specs=[
            pl.BlockSpec((tk, k), lambda j: (j, 0)),
            pl.BlockSpec((k, n), lambda j: (0, 0)),
            pl.BlockSpec((1, n), lambda j: (0, 0)),
            pl.BlockSpec((nv, k), lambda j: (0, 0)),
        ],
        out_specs=[pl.BlockSpec((tk, n), lambda j: (j, 0)), pl.BlockSpec((1, nv, tk), lambda j: (j, 0, 0))],
        out_shape=[jax.ShapeDtypeStruct((m, n), BF16), jax.ShapeDtypeStruct((m // tk, nv, tk), BF16)],
        compiler_params=_params("parallel"),
        name="proj_qkv",
    )(x2d, w_qk, qk_scale, w_v_t)


ATTN_ONES_ROWS = 16
ATTN_SKIP_LOG2 = 160.0
ALIBI_PIECES = 5


def _alibi_tables(t):
    slopes = np.exp2(-8.0 * np.arange(1, DA_HEADS + 1, dtype=np.float64) / DA_HEADS)
    ramp = float(LOG2E) * slopes[:, None] * np.arange(t, dtype=np.float64)[None, :]

    def pieces(v):
        out = []
        for _ in range(ALIBI_PIECES):
            piece = v.astype(np.float32).astype(BF16).astype(np.float64)
            out.append(piece)
            v = v - piece
        assert not v.any()
        return out

    ones = np.ones_like(ramp)
    pad = [np.zeros_like(ramp)] * (DA_V_DIM - 2 * ALIBI_PIECES)
    key_side = np.stack(pieces(ramp) + [ones] * ALIBI_PIECES + pad, axis=-1)
    query_side = np.stack([ones] * ALIBI_PIECES + pieces(-ramp) + pad, axis=-1)
    return jnp.asarray(key_side, BF16), jnp.asarray(query_side, BF16)


def _attn_kernel(slopes_ref, inv_tile_ref, lq1_ref, lk1_ref, lq2_ref, lk2_ref, sw_ref, q_ref, qb_ref, k_ref, kb_ref,
                 vt_ref, causal_ref, o_ref, sa_ref, sb_ref, m_ref, acc_ref, kmax_ref, *, t, heads, lam_init):
    hg = pl.program_id(1)
    i = pl.program_id(2)
    slope_log2 = [slopes_ref[hg * heads + g] * LOG2E for g in range(heads)]

    lane = lax.broadcasted_iota(jnp.int32, (t, DA_V_DIM), 1)
    q_maps = []
    for g in range(heads):
        q = q_ref[:, g * DA_V_DIM:(g + 1) * DA_V_DIM]
        zero = jnp.zeros_like(q)
        q_maps += [jnp.concatenate([jnp.where(lane < DA_QK_DIM, q, zero), qb_ref[g]], axis=1),
                   jnp.concatenate([jnp.where(lane >= DA_QK_DIM, q, zero), qb_ref[g]], axis=1)]
    ones_rows = jnp.ones((ATTN_ONES_ROWS, t), BF16)

    m_ref[...] = jnp.full(m_ref.shape, NEG_INF, F32)
    acc_ref[...] = jnp.zeros(acc_ref.shape, F32)

    def scores(j, s_ref):
        rows = pl.ds(pl.multiple_of(j * t, t), t)
        for g in range(heads):
            k_aug = jnp.concatenate([k_ref[rows, g * DA_V_DIM:(g + 1) * DA_V_DIM], kb_ref[g]], axis=1)
            for mp in range(2):
                s_ref[2 * g + mp] = _dot_nt(k_aug, q_maps[2 * g + mp])

    def consume(j, s_ref, diagonal=False):
        for g in range(heads):
            vt_aug = jnp.concatenate([vt_ref[j, g * DA_V_DIM:(g + 1) * DA_V_DIM, :], ones_rows], axis=0)
            shift = slope_log2[g] * ((j - i) * t).astype(F32)
            for mp in range(2):
                c = 2 * g + mp
                s = s_ref[c]
                if diagonal:
                    s = s + causal_ref[...]
                m_old = m_ref[c]
                m_new = jnp.maximum(m_old, jnp.max(s, axis=0, keepdims=True) + shift)
                p = jnp.exp2(s - (m_new - shift))
                acc_ref[c] = jnp.exp2(m_old - m_new) * acc_ref[c] + _dot(vt_aug, p.astype(BF16))
                m_ref[c] = m_new

    @pl.when(i == 0)
    def _():
        for g in range(heads):
            kmax_ref[g] = jnp.max(jnp.abs(k_ref[:, g * DA_V_DIM:(g + 1) * DA_V_DIM]).astype(F32))

    scores(i, sa_ref)
    scores(jnp.maximum(i - 1, 0), sb_ref)
    consume(i, sa_ref, diagonal=True)

    m_min = jnp.min(m_ref[...])
    keep = jnp.int32(0)
    for g in range(heads):
        q_l1 = jnp.max(jnp.sum(jnp.abs(q_ref[:, g * DA_V_DIM:(g + 1) * DA_V_DIM].astype(F32)), axis=1, keepdims=True))
        reach = (ATTN_SKIP_LOG2 + q_l1 * kmax_ref[g] - m_min) * inv_tile_ref[hg * heads + g] + (t - 1) / t
        keep = jnp.maximum(keep, jnp.minimum(reach, float(2 ** 20)).astype(jnp.int32))
    keep = jnp.clip(keep, 0, i)

    def tile_pair(jj, carry):
        dist = 1 + 2 * jj
        scores(jnp.maximum(i - dist - 1, 0), sa_ref)
        consume(i - dist, sb_ref)
        scores(jnp.maximum(i - dist - 2, 0), sb_ref)
        consume(i - dist - 1, sa_ref)
        return carry

    lax.fori_loop(0, keep // 2, tile_pair, 0)

    @pl.when(keep % 2 == 1)
    def _():
        consume(i - keep, sb_ref)

    lam = (jnp.exp(jnp.sum(lq1_ref[...] * lk1_ref[...], axis=-1, keepdims=True))
           - jnp.exp(jnp.sum(lq2_ref[...] * lk2_ref[...], axis=-1, keepdims=True)) + lam_init)
    for g in range(heads):
        a1 = acc_ref[2 * g]
        a2 = acc_ref[2 * g + 1]
        o_t = (a1[:DA_V_DIM] / a1[DA_V_DIM:DA_V_DIM + 1]
               - lam * (a2[:DA_V_DIM] / a2[DA_V_DIM:DA_V_DIM + 1]))
        o = o_t.T
        o = o * lax.rsqrt(jnp.mean(o * o, axis=-1, keepdims=True) + LN_EPS)
        o_ref[:, g * DA_V_DIM:(g + 1) * DA_V_DIM] = (o * sw_ref[...] * (1.0 - lam_init)).astype(o_ref.dtype)


def _diff_attention(qk, vtb, slopes, lq1, lk1, lq2, lk2, subln_w, batch, seq, lam_init, t, heads):
    n_tok = qk.shape[0]
    nq = seq // t
    width = heads * DA_V_DIM
    key_bias, query_bias = _alibi_tables(t)
    pos = np.arange(t)
    causal = jnp.asarray(np.where(pos[:, None] <= pos[None, :], 0.0, NEG_INF), F32)
    vec = pl.BlockSpec((1, DA_QK_DIM), lambda b, h, i: (0, 0))
    bias_spec = pl.BlockSpec((heads, t, DA_V_DIM), lambda b, h, i: (h, 0, 0))
    maps = 2 * heads
    return pl.pallas_call(
        functools.partial(_attn_kernel, t=t, heads=heads, lam_init=lam_init),
        grid=(batch, DA_HEADS // heads, nq),
        in_specs=[
            pl.BlockSpec(memory_space=pltpu.SMEM),
            pl.BlockSpec(memory_space=pltpu.SMEM),
            vec, vec, vec, vec,
            pl.BlockSpec((1, DA_V_DIM), lambda b, h, i: (0, 0)),
            pl.BlockSpec((t, width), lambda b, h, i: (b * nq + i, h)),
            bias_spec,
            pl.BlockSpec((seq, width), lambda b, h, i: (b, DA_HEADS // heads + h)),
            bias_spec,
            pl.BlockSpec((nq, width, t), lambda b, h, i: (b, h, 0)),
            pl.BlockSpec((t, t), lambda b, h, i: (0, 0)),
        ],
        out_specs=pl.BlockSpec((t, width), lambda b, h, i: (b * nq + i, h)),
        out_shape=jax.ShapeDtypeStruct((n_tok, DA_WIDTH), BF16),
        scratch_shapes=[
            pltpu.VMEM((maps, t, t), F32),
            pltpu.VMEM((maps, t, t), F32),
            pltpu.VMEM((maps, 1, t), F32),
            pltpu.VMEM((maps, DA_V_DIM + ATTN_ONES_ROWS, t), F32),
            pltpu.SMEM((heads,), F32),
        ],
        compiler_params=_params("parallel", "parallel", "arbitrary"),
        name="diff_attention",
    )(slopes, 1.0 / (slopes * (LOG2E * t)), lq1, lk1, lq2, lk2, subln_w, qk, query_bias, qk, key_bias, vtb, causal)


def _mix_kernel(x_ref, xh_ref, o_ref, wp_ref, wgp_ref, wga_ref, poolw_ref, pscale_ref, wbp_ref, wba_ref,
                wout_ref, g_ref, b_ref, out_ref, e_ref, *, tm, seq):
    i = pl.program_id(0)
    start = (i * tm) % seq
    x = x_ref[...]
    xb = x.astype(BF16)

    p = _dot(xb, wp_ref[...])
    p_halo = _dot(xh_ref[...].astype(BF16), wp_ref[...])
    e_ref[0:POOL_HALO, :] = jnp.where(start == 0, 0.0, p_halo)
    e_ref[POOL_HALO:, :] = p

    pos = (start + 1 + lax.broadcasted_iota(jnp.int32, (tm, 1), 0)).astype(F32)
    ys = []
    for g, w in enumerate(POOL_WINDOWS):
        cols = slice(g * POOL_GROUP_DIM, (g + 1) * POOL_GROUP_DIM)
        acc = e_ref[POOL_HALO:POOL_HALO + tm, cols]
        for back in range(1, w):
            acc = acc + e_ref[POOL_HALO - back:POOL_HALO - back + tm, cols]
        pooled = acc / jnp.minimum(pos, float(w)) - e_ref[POOL_HALO:POOL_HALO + tm, cols]
        ys.append(_dot(pooled.astype(BF16), poolw_ref[g]))
    y = jnp.concatenate(ys, axis=-1) * pscale_ref[...]
    y_pool = _dot(y.astype(BF16), wbp_ref[...])

    gate_p = _dot(xb, wgp_ref[...])
    gate_a = _dot(xb, wga_ref[...])
    y_attn = _dot(o_ref[...], wba_ref[...])
    merged = jax.nn.sigmoid(gate_p) * y_pool + jax.nn.sigmoid(gate_a) * y_attn
    z = DN_ALPHA * x + _dot(merged.astype(BF16), wout_ref[...])
    out_ref[...] = _layer_norm(z, g_ref[...], b_ref[...])


def _mix_ln1(x2d, o, w_p, w_gp, w_ga, pool_w, pool_scale, w_bp, w_ba, w_out, g, b, seq, tm):
    n_tok = x2d.shape[0]
    halo_blocks = tm // POOL_HALO

    def full(a):
        return pl.BlockSpec(a.shape, lambda i: (0,) * a.ndim)

    return pl.pallas_call(
        functools.partial(_mix_kernel, tm=tm, seq=seq),
        grid=(n_tok // tm,),
        in_specs=[
            pl.BlockSpec((tm, D_MODEL), lambda i: (i, 0)),
            pl.BlockSpec((POOL_HALO, D_MODEL), lambda i: (jnp.maximum(i * halo_blocks - 1, 0), 0)),
            pl.BlockSpec((tm, DA_WIDTH), lambda i: (i, 0)),
            full(w_p), full(w_gp), full(w_ga), full(pool_w), full(pool_scale), full(w_bp), full(w_ba),
            full(w_out), full(g), full(b),
        ],
        out_specs=pl.BlockSpec((tm, D_MODEL), lambda i: (i, 0)),
        out_shape=jax.ShapeDtypeStruct((n_tok, D_MODEL), F32),
        scratch_shapes=[pltpu.VMEM((tm + POOL_HALO, POOL_WIDTH), F32)],
        compiler_params=_params("parallel"),
        name="mix_ln1",
    )(x2d, x2d, o, w_p, w_gp, w_ga, pool_w, pool_scale, w_bp, w_ba, w_out, g, b)


def _xattn_kernel(x_ref, kv_ref, wq_ref, wo_ref, g_ref, b_ref, out_ref):
    x = x_ref[...]
    q = (_dot(x.astype(BF16), wq_ref[...]) * (XA_HEAD_DIM ** -0.5)).astype(BF16)
    kv = kv_ref[0]
    outs = []
    for h in range(XA_HEADS):
        cols = slice(h * XA_HEAD_DIM, (h + 1) * XA_HEAD_DIM)
        s = _dot_nt(q[:, cols], kv[:, cols])
        p = jnp.exp(s - jnp.max(s, axis=-1, keepdims=True))
        l = jnp.sum(p, axis=-1, keepdims=True)
        v = kv[:, D_MODEL + h * XA_HEAD_DIM:D_MODEL + (h + 1) * XA_HEAD_DIM]
        outs.append(_dot(p.astype(BF16), v) / l)
    o = jnp.concatenate(outs, axis=-1)
    z = DN_ALPHA * x + _dot(o.astype(BF16), wo_ref[...])
    out_ref[...] = _layer_norm(z, g_ref[...], b_ref[...])


def _xattn_ln2(x1, kvm, w_cq, w_co, g, b, seq, tm):
    n_tok = x1.shape[0]
    n_mem = kvm.shape[1]

    def full(a):
        return pl.BlockSpec(a.shape, lambda i: (0,) * a.ndim)

    return pl.pallas_call(
        _xattn_kernel,
        grid=(n_tok // tm,),
        in_specs=[
            pl.BlockSpec((tm, D_MODEL), lambda i: (i, 0)),
            pl.BlockSpec((1, n_mem, 2 * D_MODEL), lambda i: ((i * tm) // seq, 0, 0)),
            full(w_cq), full(w_co), full(g), full(b),
        ],
        out_specs=pl.BlockSpec((tm, D_MODEL), lambda i: (i, 0)),
        out_shape=jax.ShapeDtypeStruct((n_tok, D_MODEL), F32),
        compiler_params=_params("parallel"),
        name="xattn_ln2",
    )(x1, kvm, w_cq, w_co, g, b)


SUBLANES = 8


def _sort_network(n):
    def merge(lo, hi, r):
        step = 2 * r
        if step < hi - lo:
            yield from merge(lo, hi, step)
            yield from merge(lo + r, hi, step)
            yield from ((i, i + r) for i in range(lo + r, hi - r, step))
        else:
            yield (lo, lo + r)

    def sort(lo, hi):
        if hi - lo >= 1:
            mid = lo + (hi - lo) // 2
            yield from sort(lo, mid)
            yield from sort(mid + 1, hi)
            yield from merge(lo, hi, 1)

    return tuple(sort(0, n - 1))


def _exchange(v, i, j):
    if v[j] is None:
        return
    if v[i] is None:
        v[i], v[j] = v[j], None
        return
    v[i], v[j] = jnp.maximum(v[i], v[j]), jnp.minimum(v[i], v[j])


def _top_sorted(blocks, k):
    v = list(blocks) + [None] * (k - len(blocks))
    for i, j in _sort_network(k):
        _exchange(v, i, j)
    shift = SUBLANES // 2
    while shift >= 1:
        other = [None if a is None else pltpu.roll(a, shift, 0) for a in v]
        v = [b if a is None else (a if b is None else jnp.maximum(a, b)) for a, b in zip(v, reversed(other))]
        stride = k // 2
        while stride >= 1:
            for i in range(k):
                if not i & stride:
                    _exchange(v, i, i + stride)
            stride //= 2
        shift //= 2
    return v


def _route_kernel(x_ref, wq_ref, keys_ref, rank2_ref, e2_ref, count_ref, d_ref, *, heads):
    q_t = _dot_nt(wq_ref[...], x_ref[...].astype(BF16)).astype(BF16)
    for g in range(heads):
        q_g = q_t[g * PEER_QUERY_DIM:(g + 1) * PEER_QUERY_DIM]
        s1 = _dot(keys_ref[g, 0], q_g[:PEER_HALF])
        s2 = _dot(keys_ref[g, 1], q_g[PEER_HALF:])
        _route_head(s1, s2, g, rank2_ref, e2_ref, count_ref, d_ref)


def _route_head(s1, s2, g, rank2_ref, e2_ref, count_ref, d_ref):
    k = PEER_TOPK
    n_blocks = PEER_N_KEYS // SUBLANES
    s1_blocks = [s1[r * SUBLANES:(r + 1) * SUBLANES] for r in range(n_blocks)]
    s2_blocks = [s2[r * SUBLANES:(r + 1) * SUBLANES] for r in range(n_blocks)]
    v1 = _top_sorted(s1_blocks, k)
    v2 = _top_sorted(s2_blocks, k)

    row = lax.broadcasted_iota(jnp.int32, v1[0].shape, 0)

    def pack(vals):
        out = vals[0]
        for r in range(1, SUBLANES):
            out = jnp.where(row == r, vals[r], out)
        return out

    v1_lo, v1_hi, v2_hi = pack(v1[:SUBLANES]), pack(v1[SUBLANES:]), pack(v2[SUBLANES:])
    cand = ([v1_lo + v2[0], v1_hi + v2[0]] + [v1_lo + v2[b] for b in range(1, SUBLANES)] + [v2_hi + v1[0]])
    top = _top_sorted(cand, k)
    tau = top[k - 1]
    z = jnp.ones(tau.shape, F32)
    for c in top[1:]:
        z = z + jnp.exp(c - top[0])
    inv_z = 1.0 / z

    partners = []
    for a in range(k):
        n = jnp.zeros(tau.shape, F32)
        for b in range(k // (a + 1)):
            n = jnp.where(v1[a] + v2[b] >= tau, float(b + 1), n)
        partners.append(n)

    rank2, e2, count, d = [], [], [], []
    for blk1, blk2 in zip(s1_blocks, s2_blocks):
        r2 = jnp.full(blk2.shape, float(k), F32)
        cnt = jnp.zeros(blk1.shape, F32)
        for b in reversed(range(k)):
            r2 = jnp.where(blk2 >= v2[b], float(b), r2)
            cnt = jnp.where(blk1 >= v1[b], partners[b], cnt)
        rank2.append(r2)
        count.append(cnt)
        e2.append(jnp.exp(blk2 - v2[0]))
        d.append(jnp.exp(blk1 - v1[0]) * inv_z)
    rank2_ref[g] = jnp.concatenate(rank2, axis=0).astype(BF16)
    e2_ref[g] = jnp.concatenate(e2, axis=0).astype(BF16)
    count_ref[g] = jnp.concatenate(count, axis=0)
    d_ref[g] = jnp.concatenate(d, axis=0)


def _peer_route(x2, wq_t, keys, tm, heads):
    n_tok = x2.shape[0]
    out_spec = pl.BlockSpec((heads, PEER_N_KEYS, tm), lambda i, h: (h, 0, i))
    out_shape = [jax.ShapeDtypeStruct((PEER_HEADS, PEER_N_KEYS, n_tok), dt) for dt in (BF16, BF16, F32, F32)]
    return pl.pallas_call(
        functools.partial(_route_kernel, heads=heads),
        grid=(n_tok // tm, PEER_HEADS // heads),
        in_specs=[
            pl.BlockSpec((tm, D_MODEL), lambda i, h: (i, 0)),
            pl.BlockSpec((heads * PEER_QUERY_DIM, D_MODEL), lambda i, h: (h, 0)),
            pl.BlockSpec((heads, 2, PEER_N_KEYS, PEER_HALF), lambda i, h: (h, 0, 0, 0)),
        ],
        out_specs=[out_spec] * 4,
        out_shape=out_shape,
        compiler_params=_params("parallel", "parallel"),
        name="peer_route",
    )(x2, wq_t, keys)


def _peer_kernel(x_ref, u_ref, vt_ref, rank2_ref, e2_ref, count_ref, d_ref, g_ref, b_ref, out_ref,
                 xb_ref, acc_ref, a_ref, *, chunks, rows_per_chunk):
    e = pl.program_id(1)

    @pl.when(e == 0)
    def _():
        xb_ref[...] = (x_ref[...] * SQRT_HALF).astype(BF16)
        acc_ref[...] = jnp.zeros(acc_ref.shape, F32)

    xb = xb_ref[...]
    chunk = rows_per_chunk * PEER_N_KEYS
    tb = xb.shape[0]

    def sublane_rows(ref, h, row):
        tile = jnp.broadcast_to(ref[h, row:row + 1, :], (16, tb)).astype(BF16)
        return jnp.concatenate([tile] * (PEER_N_KEYS // 16), axis=0)

    def activations(c):
        return _dot_nt(u_ref[c * chunk:(c + 1) * chunk, :], xb)

    a_ref[0] = activations(0)
    for c in range(chunks):
        if c + 1 < chunks:
            a_ref[(c + 1) % 2] = activations(c + 1)
        parts = []
        for r in range(rows_per_chunk):
            row = c * rows_per_chunk + r
            z = a_ref[c % 2, r * PEER_N_KEYS:(r + 1) * PEER_N_KEYS, :]
            act = z + z * lax.erf(z)
            gate = None
            for h in range(PEER_HEADS):
                cnt = sublane_rows(count_ref, h, row)
                d = sublane_rows(d_ref, h, row)
                term = jnp.where(rank2_ref[h] < cnt, e2_ref[h], jnp.zeros((), BF16)) * d
                gate = term if gate is None else gate + term
            parts.append(gate * act.astype(BF16))
        w = jnp.concatenate(parts, axis=0)
        acc_ref[...] += _dot(vt_ref[:, c * chunk:(c + 1) * chunk], w)

    @pl.when(e == pl.num_programs(1) - 1)
    def _():
        z = DN_ALPHA * x_ref[...] + SQRT_HALF * acc_ref[...].T
        out_ref[...] = _layer_norm(z, g_ref[...], b_ref[...])


def _peer_ln3(x2, u, vt, rank2, e2, count, d, g, b, tb, te, chunk):
    n_tok = x2.shape[0]
    rows_per_tile = te // PEER_N_KEYS
    key_spec = pl.BlockSpec((PEER_HEADS, PEER_N_KEYS, tb), lambda t, e: (0, 0, t))
    row_spec = pl.BlockSpec((PEER_HEADS, rows_per_tile, tb), lambda t, e: (0, e, t))
    vec = pl.BlockSpec((1, D_MODEL), lambda t, e: (0, 0))
    return pl.pallas_call(
        functools.partial(_peer_kernel, chunks=te // chunk, rows_per_chunk=chunk // PEER_N_KEYS),
        grid=(n_tok // tb, PEER_N_EXPERTS // te),
        in_specs=[
            pl.BlockSpec((tb, D_MODEL), lambda t, e: (t, 0)),
            pl.BlockSpec((te, D_MODEL), lambda t, e: (e, 0)),
            pl.BlockSpec((D_MODEL, te), lambda t, e: (0, e)),
            key_spec, key_spec, row_spec, row_spec,
            vec, vec,
        ],
        out_specs=pl.BlockSpec((tb, D_MODEL), lambda t, e: (t, 0)),
        out_shape=jax.ShapeDtypeStruct((n_tok, D_MODEL), F32),
        scratch_shapes=[
            pltpu.VMEM((tb, D_MODEL), BF16),
            pltpu.VMEM((D_MODEL, tb), F32),
            pltpu.VMEM((2, chunk, tb), F32),
        ],
        compiler_params=_params("parallel", "arbitrary"),
        name="peer_dense",
    )(x2, u, vt, rank2, e2, count, d, g, b)


def kernel(x, mem, w_in, pool_w, pool_scale, w_br_pool, lambda_q1, lambda_k1, lambda_q2, lambda_k2, subln_w, w_br_attn, w_out, ln1_g, ln1_b, w_cq, w_ckv, w_co, ln2_g, ln2_b, w_pq, sub_keys, expert_u, expert_v, ln3_g, ln3_b):
    batch, seq, _ = x.shape
    n_tok = batch * seq
    n_mem = mem.shape[1]
    attn_tile = 512
    slopes = jnp.exp2(-8.0 * jnp.arange(1, DA_HEADS + 1, dtype=F32) / DA_HEADS)

    def row(v):
        return v.reshape(1, -1).astype(F32)

    for l in range(DEPTH):
        lam_init = 0.8 - 0.6 * math.exp(-0.3 * l)
        x2d = x.reshape(n_tok, D_MODEL)
        w_in_b = w_in[l].astype(BF16)
        q0 = POOL_WIDTH
        k0 = q0 + DA_WIDTH
        v0 = k0 + DA_WIDTH
        gp0 = v0 + DA_WIDTH
        ga0 = gp0 + D_MODEL

        qk_scale = jnp.concatenate([jnp.full((1, DA_WIDTH), LOG2E * DA_QK_DIM ** -0.5, F32),
                                    jnp.ones((1, DA_WIDTH), F32)], axis=1)
        qk, vtb = _proj_qkv(x2d, w_in_b[:, q0:v0], qk_scale, w_in_b[:, v0:gp0].T, attn_tile)
        o = _diff_attention(qk, vtb, slopes, row(lambda_q1[l]), row(lambda_k1[l]), row(lambda_q2[l]),
                            row(lambda_k2[l]), row(subln_w[l]), batch, seq, lam_init, attn_tile, 2)

        x1 = _mix_ln1(x2d, o, w_in_b[:, :POOL_WIDTH], w_in_b[:, gp0:ga0], w_in_b[:, ga0:],
                      pool_w[l].astype(BF16), row(pool_scale[l]), w_br_pool[l].astype(BF16),
                      w_br_attn[l].astype(BF16), w_out[l].astype(BF16), row(ln1_g[l]), row(ln1_b[l]), seq, 512)

        kvm = _proj(mem.reshape(batch * n_mem, D_MODEL), w_ckv[l].astype(BF16), 0, jnp.ones((1, 2 * D_MODEL), F32),
                    BF16, batch * n_mem, 512, "proj_mem_kv").reshape(batch, n_mem, 2 * D_MODEL)
        x2 = _xattn_ln2(x1, kvm, w_cq[l].astype(BF16), w_co[l].astype(BF16), row(ln2_g[l]), row(ln2_b[l]), seq, 512)

        rank2, e2, count, d = _peer_route(x2, w_pq[l].T.astype(BF16), sub_keys[l].astype(BF16), 512, 4)
        x3 = _peer_ln3(x2, expert_u[l].astype(BF16), expert_v[l].T.astype(BF16), rank2, e2, count, d,
                       row(ln3_g[l]), row(ln3_b[l]), 512, 2048, 1024)
        x = x3.reshape(batch, seq, D_MODEL)
    return x
```

```python
import functools
import math
from typing import NamedTuple

import jax
import jax.numpy as jnp
import numpy as np
from jax import lax
from jax.experimental import pallas as pl
from jax.experimental.pallas import tpu as pltpu

F32 = jnp.float32
BF16 = jnp.bfloat16

D_MODEL = 1024
POOL_WINDOWS = (2, 4, 8, 16)
POOL_GROUPS = len(POOL_WINDOWS)
POOL_WIDTH = D_MODEL // 2
POOL_GROUP_DIM = POOL_WIDTH // POOL_GROUPS
POOL_HALO = max(POOL_WINDOWS)
DA_HEADS = 8
DA_QK_DIM = 64
DA_V_DIM = 2 * DA_QK_DIM
DA_WIDTH = DA_HEADS * DA_V_DIM
XA_HEADS = 4
XA_HEAD_DIM = D_MODEL // XA_HEADS
PEER_HEADS = 8
PEER_N_KEYS = 128
PEER_N_EXPERTS = PEER_N_KEYS * PEER_N_KEYS
PEER_QUERY_DIM = 128
PEER_HALF = PEER_QUERY_DIM // 2
PEER_TOPK = 16
DEPTH = 1
DN_ALPHA = (2 * DEPTH) ** 0.25
LN_EPS = 1e-5
NEG_INF = -1e30
SQRT_HALF = math.sqrt(0.5)
LOG2E = float(np.float32(math.log2(math.e)))

VMEM_LIMIT_BYTES = 52 * 1024 * 1024
BF16_SUBLANES = 16


class Tiles(NamedTuple):
    attn: int = 512
    attn_heads: int = 2
    mix: int = 512
    mem_cols: int = 512
    peer_tokens: int = 512
    route_heads: int = 4
    peer_experts: int = 2048
    peer_chunk: int = 1024


TILES = Tiles()

_NT = (((1,), (1,)), ((), ()))


def _params(*sem):
    return pltpu.CompilerParams(dimension_semantics=sem, vmem_limit_bytes=VMEM_LIMIT_BYTES)


def _dot(a, b):
    return jnp.dot(a, b, preferred_element_type=F32)


def _dot_nt(a, b):
    return lax.dot_general(a, b, _NT, preferred_element_type=F32)


def _layer_norm(z, g, b):
    mu = jnp.mean(z, axis=-1, keepdims=True)
    zc = z - mu
    var = jnp.mean(zc * zc, axis=-1, keepdims=True)
    return zc * lax.rsqrt(var + LN_EPS) * g + b


def _proj_kernel(a_ref, w_ref, o_ref):
    o_ref[...] = _dot(a_ref[...].astype(BF16), w_ref[...]).astype(o_ref.dtype)


def _proj(a, w, tn, name):
    m, k = a.shape
    n = w.shape[1]
    return pl.pallas_call(
        _proj_kernel,
        grid=(n // tn,),
        in_specs=[pl.BlockSpec((m, k), lambda j: (0, 0)), pl.BlockSpec((k, tn), lambda j: (0, j))],
        out_specs=pl.BlockSpec((m, tn), lambda j: (0, j)),
        out_shape=jax.ShapeDtypeStruct((m, n), BF16),
        compiler_params=_params("parallel"),
        name=name,
    )(a, w)


def _proj_qkv_kernel(x_ref, wqk_ref, scale_ref, wvt_ref, qk_ref, vt_ref):
    xb = x_ref[...].astype(BF16)
    qk_ref[...] = (_dot(xb, wqk_ref[...]) * scale_ref[...]).astype(qk_ref.dtype)
    vt_ref[0] = _dot_nt(wvt_ref[...], xb).astype(vt_ref.dtype)


def _proj_qkv(x2d, w_qk, qk_scale, w_v_t, tk):
    m, k = x2d.shape
    n = w_qk.shape[1]
    nv = w_v_t.shape[0]
    return pl.pallas_call(
        _proj_qkv_kernel,
        grid=(m // tk,),
        in_specs=[
            pl.BlockSpec((tk, k), lambda j: (j, 0)),
            pl.BlockSpec((k, n), lambda j: (0, 0)),
            pl.BlockSpec((1, n), lambda j: (0, 0)),
            pl.BlockSpec((nv, k), lambda j: (0, 0)),
        ],
        out_specs=[pl.BlockSpec((tk, n), lambda j: (j, 0)), pl.BlockSpec((1, nv, tk), lambda j: (j, 0, 0))],
        out_shape=[jax.ShapeDtypeStruct((m, n), BF16), jax.ShapeDtypeStruct((m // tk, nv, tk), BF16)],
        compiler_params=_params("parallel"),
        name="proj_qkv",
    )(x2d, w_qk, qk_scale, w_v_t)


ATTN_ONES_ROWS = 16
ATTN_SKIP_LOG2 = 160.0
ALIBI_PIECES = 5


def _alibi_tables(t):
    slopes = np.exp2(-8.0 * np.arange(1, DA_HEADS + 1, dtype=np.float64) / DA_HEADS)
    ramp = float(LOG2E) * slopes[:, None] * np.arange(t, dtype=np.float64)[None, :]

    def pieces(v):
        out = []
        for _ in range(ALIBI_PIECES):
            piece = v.astype(np.float32).astype(BF16).astype(np.float64)
            out.append(piece)
            v = v - piece
        assert not v.any()
        return out

    ones = np.ones_like(ramp)
    pad = [np.zeros_like(ramp)] * (DA_V_DIM - 2 * ALIBI_PIECES)
    key_side = np.stack(pieces(ramp) + [ones] * ALIBI_PIECES + pad, axis=-1)
    query_side = np.stack([ones] * ALIBI_PIECES + pieces(-ramp) + pad, axis=-1)
    return jnp.asarray(key_side, BF16), jnp.asarray(query_side, BF16)


def _attn_kernel(slopes_ref, inv_tile_ref, lq1_ref, lk1_ref, lq2_ref, lk2_ref, sw_ref, q_ref, qb_ref, k_ref, kb_ref,
                 vt_ref, causal_ref, o_ref, sa_ref, sb_ref, m_ref, acc_ref, kmax_ref, *, t, heads, lam_init):
    hg = pl.program_id(1)
    i = pl.program_id(2)
    slope_log2 = [slopes_ref[hg * heads + g] * LOG2E for g in range(heads)]

    lane = lax.broadcasted_iota(jnp.int32, (t, DA_V_DIM), 1)
    q_maps = []
    for g in range(heads):
        q = q_ref[:, g * DA_V_DIM:(g + 1) * DA_V_DIM]
        zero = jnp.zeros_like(q)
        q_maps += [jnp.concatenate([jnp.where(lane < DA_QK_DIM, q, zero), qb_ref[g]], axis=1),
                   jnp.concatenate([jnp.where(lane >= DA_QK_DIM, q, zero), qb_ref[g]], axis=1)]
    ones_rows = jnp.ones((ATTN_ONES_ROWS, t), BF16)

    m_ref[...] = jnp.full(m_ref.shape, NEG_INF, F32)
    acc_ref[...] = jnp.zeros(acc_ref.shape, F32)

    def scores(j, s_ref):
        rows = pl.ds(pl.multiple_of(j * t, t), t)
        for g in range(heads):
            k_aug = jnp.concatenate([k_ref[rows, g * DA_V_DIM:(g + 1) * DA_V_DIM], kb_ref[g]], axis=1)
            for mp in range(2):
                s_ref[2 * g + mp] = _dot_nt(k_aug, q_maps[2 * g + mp])

    def consume(j, s_ref, diagonal=False):
        for g in range(heads):
            vt_aug = jnp.concatenate([vt_ref[j, g * DA_V_DIM:(g + 1) * DA_V_DIM, :], ones_rows], axis=0)
            shift = slope_log2[g] * ((j - i) * t).astype(F32)
            for mp in range(2):
                c = 2 * g + mp
                s = s_ref[c]
                if diagonal:
                    s = s + causal_ref[...]
                m_old = m_ref[c]
                m_new = jnp.maximum(m_old, jnp.max(s, axis=0, keepdims=True) + shift)
                p = jnp.exp2(s - (m_new - shift))
                acc_ref[c] = jnp.exp2(m_old - m_new) * acc_ref[c] + _dot(vt_aug, p.astype(BF16))
                m_ref[c] = m_new

    @pl.when(i == 0)
    def _():
        for g in range(heads):
            kmax_ref[g] = jnp.max(jnp.abs(k_ref[:, g * DA_V_DIM:(g + 1) * DA_V_DIM]).astype(F32))

    scores(i, sa_ref)
    scores(jnp.maximum(i - 1, 0), sb_ref)
    consume(i, sa_ref, diagonal=True)

    m_min = jnp.min(m_ref[...])
    keep = jnp.int32(0)
    for g in range(heads):
        q_l1 = jnp.max(jnp.sum(jnp.abs(q_ref[:, g * DA_V_DIM:(g + 1) * DA_V_DIM].astype(F32)), axis=1, keepdims=True))
        reach = (ATTN_SKIP_LOG2 + q_l1 * kmax_ref[g] - m_min) * inv_tile_ref[hg * heads + g] + (t - 1) / t
        keep = jnp.maximum(keep, jnp.minimum(reach, float(2 ** 20)).astype(jnp.int32))
    keep = jnp.clip(keep, 0, i)

    def tile_pair(jj, carry):
        dist = 1 + 2 * jj
        scores(jnp.maximum(i - dist - 1, 0), sa_ref)
        consume(i - dist, sb_ref)
        scores(jnp.maximum(i - dist - 2, 0), sb_ref)
        consume(i - dist - 1, sa_ref)
        return carry

    lax.fori_loop(0, keep // 2, tile_pair, 0)

    @pl.when(keep % 2 == 1)
    def _():
        consume(i - keep, sb_ref)

    lam = (jnp.exp(jnp.sum(lq1_ref[...] * lk1_ref[...], axis=-1, keepdims=True))
           - jnp.exp(jnp.sum(lq2_ref[...] * lk2_ref[...], axis=-1, keepdims=True)) + lam_init)
    for g in range(heads):
        a1 = acc_ref[2 * g]
        a2 = acc_ref[2 * g + 1]
        o_t = (a1[:DA_V_DIM] / a1[DA_V_DIM:DA_V_DIM + 1]
               - lam * (a2[:DA_V_DIM] / a2[DA_V_DIM:DA_V_DIM + 1]))
        o = o_t.T
        o = o * lax.rsqrt(jnp.mean(o * o, axis=-1, keepdims=True) + LN_EPS)
        o_ref[:, g * DA_V_DIM:(g + 1) * DA_V_DIM] = (o * sw_ref[...] * (1.0 - lam_init)).astype(o_ref.dtype)


def _diff_attention(qk, vtb, slopes, lq1, lk1, lq2, lk2, subln_w, batch, seq, lam_init, t, heads):
    n_tok = qk.shape[0]
    nq = seq // t
    width = heads * DA_V_DIM
    key_bias, query_bias = _alibi_tables(t)
    pos = np.arange(t)
    causal = jnp.asarray(np.where(pos[:, None] <= pos[None, :], 0.0, NEG_INF), F32)
    vec = pl.BlockSpec((1, DA_QK_DIM), lambda b, h, i: (0, 0))
    bias_spec = pl.BlockSpec((heads, t, DA_V_DIM), lambda b, h, i: (h, 0, 0))
    maps = 2 * heads
    return pl.pallas_call(
        functools.partial(_attn_kernel, t=t, heads=heads, lam_init=lam_init),
        grid=(batch, DA_HEADS // heads, nq),
        in_specs=[
            pl.BlockSpec(memory_space=pltpu.SMEM),
            pl.BlockSpec(memory_space=pltpu.SMEM),
            vec, vec, vec, vec,
            pl.BlockSpec((1, DA_V_DIM), lambda b, h, i: (0, 0)),
            pl.BlockSpec((t, width), lambda b, h, i: (b * nq + i, h)),
            bias_spec,
            pl.BlockSpec((seq, width), lambda b, h, i: (b, DA_HEADS // heads + h)),
            bias_spec,
            pl.BlockSpec((nq, width, t), lambda b, h, i: (b, h, 0)),
            pl.BlockSpec((t, t), lambda b, h, i: (0, 0)),
        ],
        out_specs=pl.BlockSpec((t, width), lambda b, h, i: (b * nq + i, h)),
        out_shape=jax.ShapeDtypeStruct((n_tok, DA_WIDTH), BF16),
        scratch_shapes=[
            pltpu.VMEM((maps, t, t), F32),
            pltpu.VMEM((maps, t, t), F32),
            pltpu.VMEM((maps, 1, t), F32),
            pltpu.VMEM((maps, DA_V_DIM + ATTN_ONES_ROWS, t), F32),
            pltpu.SMEM((heads,), F32),
        ],
        compiler_params=_params("parallel", "parallel", "arbitrary"),
        name="diff_attention",
    )(slopes, 1.0 / (slopes * (LOG2E * t)), lq1, lk1, lq2, lk2, subln_w, qk, query_bias, qk, key_bias, vtb, causal)


def _mix_kernel(x_ref, xh_ref, o_ref, kv_ref, wp_ref, wgp_ref, wga_ref, poolw_ref, pscale_ref, wbp_ref, wba_ref,
                wout_ref, g_ref, b_ref, wq_ref, wo_ref, g2_ref, b2_ref, out_ref, e_ref, *, tm, seq):
    i = pl.program_id(0)
    start = (i * tm) % seq
    x = x_ref[...]
    xb = x.astype(BF16)

    p = _dot(xb, wp_ref[...])
    p_halo = _dot(xh_ref[...].astype(BF16), wp_ref[...])
    e_ref[0:POOL_HALO, :] = jnp.where(start == 0, 0.0, p_halo)
    e_ref[POOL_HALO:, :] = p

    pos = (start + 1 + lax.broadcasted_iota(jnp.int32, (tm, 1), 0)).astype(F32)
    ys = []
    for g, w in enumerate(POOL_WINDOWS):
        cols = slice(g * POOL_GROUP_DIM, (g + 1) * POOL_GROUP_DIM)
        acc = e_ref[POOL_HALO:POOL_HALO + tm, cols]
        for back in range(1, w):
            acc = acc + e_ref[POOL_HALO - back:POOL_HALO - back + tm, cols]
        pooled = acc / jnp.minimum(pos, float(w)) - e_ref[POOL_HALO:POOL_HALO + tm, cols]
        ys.append(_dot(pooled.astype(BF16), poolw_ref[g]))
    y = jnp.concatenate(ys, axis=-1) * pscale_ref[...]
    y_pool = _dot(y.astype(BF16), wbp_ref[...])

    gate_p = _dot(xb, wgp_ref[...])
    gate_a = _dot(xb, wga_ref[...])
    y_attn = _dot(o_ref[...], wba_ref[...])
    merged = jax.nn.sigmoid(gate_p) * y_pool + jax.nn.sigmoid(gate_a) * y_attn
    z = DN_ALPHA * x + _dot(merged.astype(BF16), wout_ref[...])
    x1 = _layer_norm(z, g_ref[...], b_ref[...])
    out_ref[...] = _xattn(x1, kv_ref, wq_ref, wo_ref, g2_ref, b2_ref)


def _mix_xattn(x2d, o, kvm, w_p, w_gp, w_ga, pool_w, pool_scale, w_bp, w_ba, w_out, g1, b1, w_cq, w_co, g2, b2, seq, tm):
    n_tok = x2d.shape[0]
    n_mem = kvm.shape[1]
    halo_blocks = tm // POOL_HALO

    def full(a):
        return pl.BlockSpec(a.shape, lambda i: (0,) * a.ndim)

    weights = (w_p, w_gp, w_ga, pool_w, pool_scale, w_bp, w_ba, w_out, g1, b1, w_cq, w_co, g2, b2)
    return pl.pallas_call(
        functools.partial(_mix_kernel, tm=tm, seq=seq),
        grid=(n_tok // tm,),
        in_specs=[
            pl.BlockSpec((tm, D_MODEL), lambda i: (i, 0)),
            pl.BlockSpec((POOL_HALO, D_MODEL), lambda i: (jnp.maximum(i * halo_blocks - 1, 0), 0)),
            pl.BlockSpec((tm, DA_WIDTH), lambda i: (i, 0)),
            pl.BlockSpec((1, n_mem, 2 * D_MODEL), lambda i: ((i * tm) // seq, 0, 0)),
        ] + [full(w) for w in weights],
        out_specs=pl.BlockSpec((tm, D_MODEL), lambda i: (i, 0)),
        out_shape=jax.ShapeDtypeStruct((n_tok, D_MODEL), F32),
        scratch_shapes=[pltpu.VMEM((tm + POOL_HALO, POOL_WIDTH), F32)],
        compiler_params=_params("parallel"),
        name="mix_xattn",
    )(x2d, x2d, o, kvm, *weights)


def _xattn(x, kv_ref, wq_ref, wo_ref, g_ref, b_ref):
    q = (_dot(x.astype(BF16), wq_ref[...]) * (XA_HEAD_DIM ** -0.5)).astype(BF16)
    kv = kv_ref[0]
    outs = []
    for h in range(XA_HEADS):
        cols = slice(h * XA_HEAD_DIM, (h + 1) * XA_HEAD_DIM)
        s = _dot_nt(q[:, cols], kv[:, cols])
        p = jnp.exp(s - jnp.max(s, axis=-1, keepdims=True))
        l = jnp.sum(p, axis=-1, keepdims=True)
        v = kv[:, D_MODEL + h * XA_HEAD_DIM:D_MODEL + (h + 1) * XA_HEAD_DIM]
        outs.append(_dot(p.astype(BF16), v) / l)
    o = jnp.concatenate(outs, axis=-1)
    z = DN_ALPHA * x + _dot(o.astype(BF16), wo_ref[...])
    return _layer_norm(z, g_ref[...], b_ref[...])


SUBLANES = 8


def _sort_network(n):
    def merge(lo, hi, r):
        step = 2 * r
        if step < hi - lo:
            yield from merge(lo, hi, step)
            yield from merge(lo + r, hi, step)
            yield from ((i, i + r) for i in range(lo + r, hi - r, step))
        else:
            yield (lo, lo + r)

    def sort(lo, hi):
        if hi - lo >= 1:
            mid = lo + (hi - lo) // 2
            yield from sort(lo, mid)
            yield from sort(mid + 1, hi)
            yield from merge(lo, hi, 1)

    return tuple(sort(0, n - 1))


def _exchange(v, i, j):
    if v[j] is None:
        return
    if v[i] is None:
        v[i], v[j] = v[j], None
        return
    v[i], v[j] = jnp.maximum(v[i], v[j]), jnp.minimum(v[i], v[j])


def _top_sorted(blocks, k):
    v = list(blocks) + [None] * (k - len(blocks))
    for i, j in _sort_network(k):
        _exchange(v, i, j)
    shift = SUBLANES // 2
    while shift >= 1:
        other = [None if a is None else pltpu.roll(a, shift, 0) for a in v]
        v = [b if a is None else (a if b is None else jnp.maximum(a, b)) for a, b in zip(v, reversed(other))]
        stride = k // 2
        while stride >= 1:
            for i in range(k):
                if not i & stride:
                    _exchange(v, i, i + stride)
            stride //= 2
        shift //= 2
    return v


def _route_kernel(x_ref, wq_ref, keys_ref, rank2_ref, e2_ref, count_ref, d_ref, *, heads):
    q_t = _dot_nt(wq_ref[...], x_ref[...].astype(BF16)).astype(BF16)
    for g in range(heads):
        q_g = q_t[g * PEER_QUERY_DIM:(g + 1) * PEER_QUERY_DIM]
        s1 = _dot(keys_ref[g, 0], q_g[:PEER_HALF])
        s2 = _dot(keys_ref[g, 1], q_g[PEER_HALF:])
        _route_head(s1, s2, g, rank2_ref, e2_ref, count_ref, d_ref)


def _route_head(s1, s2, g, rank2_ref, e2_ref, count_ref, d_ref):
    k = PEER_TOPK
    n_blocks = PEER_N_KEYS // SUBLANES
    s1_blocks = [s1[r * SUBLANES:(r + 1) * SUBLANES] for r in range(n_blocks)]
    s2_blocks = [s2[r * SUBLANES:(r + 1) * SUBLANES] for r in range(n_blocks)]
    v1 = _top_sorted(s1_blocks, k)
    v2 = _top_sorted(s2_blocks, k)

    row = lax.broadcasted_iota(jnp.int32, v1[0].shape, 0)

    def pack(vals):
        out = vals[0]
        for r in range(1, SUBLANES):
            out = jnp.where(row == r, vals[r], out)
        return out

    v1_lo, v1_hi, v2_hi = pack(v1[:SUBLANES]), pack(v1[SUBLANES:]), pack(v2[SUBLANES:])
    cand = ([v1_lo + v2[0], v1_hi + v2[0]] + [v1_lo + v2[b] for b in range(1, SUBLANES)] + [v2_hi + v1[0]])
    top = _top_sorted(cand, k)
    tau = top[k - 1]
    z = jnp.ones(tau.shape, F32)
    for c in top[1:]:
        z = z + jnp.exp(c - top[0])
    inv_z = 1.0 / z

    partners = []
    for a in range(k):
        n = jnp.zeros(tau.shape, F32)
        for b in range(k // (a + 1)):
            n = jnp.where(v1[a] + v2[b] >= tau, float(b + 1), n)
        partners.append(n)

    rank2, e2, count, d = [], [], [], []
    for blk1, blk2 in zip(s1_blocks, s2_blocks):
        r2 = jnp.full(blk2.shape, float(k), F32)
        cnt = jnp.zeros(blk1.shape, F32)
        for b in reversed(range(k)):
            r2 = jnp.where(blk2 >= v2[b], float(b), r2)
            cnt = jnp.where(blk1 >= v1[b], partners[b], cnt)
        rank2.append(r2)
        count.append(cnt)
        e2.append(jnp.exp(blk2 - v2[0]))
        d.append(jnp.exp(blk1 - v1[0]) * inv_z)
    rank2_ref[g] = jnp.concatenate(rank2, axis=0).astype(BF16)
    e2_ref[g] = jnp.concatenate(e2, axis=0).astype(BF16)
    count_ref[g] = jnp.concatenate(count, axis=0)
    d_ref[g] = jnp.concatenate(d, axis=0)


def _peer_route(x2, wq_t, keys, tm, heads):
    n_tok = x2.shape[0]
    out_spec = pl.BlockSpec((heads, PEER_N_KEYS, tm), lambda i, h: (h, 0, i))
    out_shape = [jax.ShapeDtypeStruct((PEER_HEADS, PEER_N_KEYS, n_tok), dt) for dt in (BF16, BF16, F32, F32)]
    return pl.pallas_call(
        functools.partial(_route_kernel, heads=heads),
        grid=(n_tok // tm, PEER_HEADS // heads),
        in_specs=[
            pl.BlockSpec((tm, D_MODEL), lambda i, h: (i, 0)),
            pl.BlockSpec((heads * PEER_QUERY_DIM, D_MODEL), lambda i, h: (h, 0)),
            pl.BlockSpec((heads, 2, PEER_N_KEYS, PEER_HALF), lambda i, h: (h, 0, 0, 0)),
        ],
        out_specs=[out_spec] * 4,
        out_shape=out_shape,
        compiler_params=_params("parallel", "parallel"),
        name="peer_route",
    )(x2, wq_t, keys)


def _peer_kernel(x_ref, u_ref, vt_ref, rank2_ref, e2_ref, count_ref, d_ref, g_ref, b_ref, out_ref,
                 xb_ref, acc_ref, a_ref, *, chunks, rows_per_chunk):
    e = pl.program_id(1)

    @pl.when(e == 0)
    def _():
        xb_ref[...] = (x_ref[...] * SQRT_HALF).astype(BF16)
        acc_ref[...] = jnp.zeros(acc_ref.shape, F32)

    xb = xb_ref[...]
    chunk = rows_per_chunk * PEER_N_KEYS
    tb = xb.shape[0]

    def sublane_rows(ref, h, row):
        tile = jnp.broadcast_to(ref[h, row:row + 1, :], (BF16_SUBLANES, tb)).astype(BF16)
        return jnp.concatenate([tile] * (PEER_N_KEYS // BF16_SUBLANES), axis=0)

    def activations(c):
        return _dot_nt(u_ref[c * chunk:(c + 1) * chunk, :], xb)

    a_ref[0] = activations(0)
    for c in range(chunks):
        if c + 1 < chunks:
            a_ref[(c + 1) % 2] = activations(c + 1)
        parts = []
        for r in range(rows_per_chunk):
            row = c * rows_per_chunk + r
            z = a_ref[c % 2, r * PEER_N_KEYS:(r + 1) * PEER_N_KEYS, :]
            act = z + z * lax.erf(z)
            gate = None
            for h in range(PEER_HEADS):
                cnt = sublane_rows(count_ref, h, row)
                d = sublane_rows(d_ref, h, row)
                term = jnp.where(rank2_ref[h] < cnt, e2_ref[h], jnp.zeros((), BF16)) * d
                gate = term if gate is None else gate + term
            parts.append(gate * act.astype(BF16))
        w = jnp.concatenate(parts, axis=0)
        acc_ref[...] += _dot(vt_ref[:, c * chunk:(c + 1) * chunk], w)

    @pl.when(e == pl.num_programs(1) - 1)
    def _():
        z = DN_ALPHA * x_ref[...] + SQRT_HALF * acc_ref[...].T
        out_ref[...] = _layer_norm(z, g_ref[...], b_ref[...])


def _peer_ln3(x2, u, vt, rank2, e2, count, d, g, b, tb, te, chunk):
    n_tok = x2.shape[0]
    rows_per_tile = te // PEER_N_KEYS
    key_spec = pl.BlockSpec((PEER_HEADS, PEER_N_KEYS, tb), lambda t, e: (0, 0, t))
    row_spec = pl.BlockSpec((PEER_HEADS, rows_per_tile, tb), lambda t, e: (0, e, t))
    vec = pl.BlockSpec((1, D_MODEL), lambda t, e: (0, 0))
    return pl.pallas_call(
        functools.partial(_peer_kernel, chunks=te // chunk, rows_per_chunk=chunk // PEER_N_KEYS),
        grid=(n_tok // tb, PEER_N_EXPERTS // te),
        in_specs=[
            pl.BlockSpec((tb, D_MODEL), lambda t, e: (t, 0)),
            pl.BlockSpec((te, D_MODEL), lambda t, e: (e, 0)),
            pl.BlockSpec((D_MODEL, te), lambda t, e: (0, e)),
            key_spec, key_spec, row_spec, row_spec,
            vec, vec,
        ],
        out_specs=pl.BlockSpec((tb, D_MODEL), lambda t, e: (t, 0)),
        out_shape=jax.ShapeDtypeStruct((n_tok, D_MODEL), F32),
        scratch_shapes=[
            pltpu.VMEM((tb, D_MODEL), BF16),
            pltpu.VMEM((D_MODEL, tb), F32),
            pltpu.VMEM((2, chunk, tb), F32),
        ],
        compiler_params=_params("parallel", "arbitrary"),
        name="peer_dense",
    )(x2, u, vt, rank2, e2, count, d, g, b)


def kernel(x, mem, w_in, pool_w, pool_scale, w_br_pool, lambda_q1, lambda_k1, lambda_q2, lambda_k2, subln_w, w_br_attn, w_out, ln1_g, ln1_b, w_cq, w_ckv, w_co, ln2_g, ln2_b, w_pq, sub_keys, expert_u, expert_v, ln3_g, ln3_b):
    batch, seq, _ = x.shape
    n_tok = batch * seq
    n_mem = mem.shape[1]
    tiles = TILES
    assert seq % tiles.attn == 0 and seq % tiles.mix == 0 and n_tok % tiles.peer_tokens == 0
    slopes = jnp.exp2(-8.0 * jnp.arange(1, DA_HEADS + 1, dtype=F32) / DA_HEADS)

    def row(v):
        return v.reshape(1, -1).astype(F32)

    for l in range(DEPTH):
        lam_init = 0.8 - 0.6 * math.exp(-0.3 * l)
        x2d = x.reshape(n_tok, D_MODEL)
        w_in_b = w_in[l].astype(BF16)
        q0 = POOL_WIDTH
        k0 = q0 + DA_WIDTH
        v0 = k0 + DA_WIDTH
        gp0 = v0 + DA_WIDTH
        ga0 = gp0 + D_MODEL

        qk_scale = jnp.concatenate([jnp.full((1, DA_WIDTH), LOG2E * DA_QK_DIM ** -0.5, F32),
                                    jnp.ones((1, DA_WIDTH), F32)], axis=1)
        qk, vtb = _proj_qkv(x2d, w_in_b[:, q0:v0], qk_scale, w_in_b[:, v0:gp0].T, tiles.attn)
        o = _diff_attention(qk, vtb, slopes, row(lambda_q1[l]), row(lambda_k1[l]), row(lambda_q2[l]),
                            row(lambda_k2[l]), row(subln_w[l]), batch, seq, lam_init, tiles.attn, tiles.attn_heads)

        kvm = _proj(mem.reshape(batch * n_mem, D_MODEL), w_ckv[l].astype(BF16), tiles.mem_cols,
                    "proj_mem_kv").reshape(batch, n_mem, 2 * D_MODEL)
        x2 = _mix_xattn(x2d, o, kvm, w_in_b[:, :POOL_WIDTH], w_in_b[:, gp0:ga0], w_in_b[:, ga0:],
                        pool_w[l].astype(BF16), row(pool_scale[l]), w_br_pool[l].astype(BF16),
                        w_br_attn[l].astype(BF16), w_out[l].astype(BF16), row(ln1_g[l]), row(ln1_b[l]),
                        w_cq[l].astype(BF16), w_co[l].astype(BF16), row(ln2_g[l]), row(ln2_b[l]), seq, tiles.mix)

        rank2, e2, count, d = _peer_route(x2, w_pq[l].T.astype(BF16), sub_keys[l].astype(BF16),
                                          tiles.peer_tokens, tiles.route_heads)
        x3 = _peer_ln3(x2, expert_u[l].astype(BF16), expert_v[l].T.astype(BF16), rank2, e2, count, d,
                       row(ln3_g[l]), row(ln3_b[l]), tiles.peer_tokens, tiles.peer_experts, tiles.peer_chunk)
        x = x3.reshape(batch, seq, D_MODEL)
    return x
```

```python
import functools
import math
from typing import NamedTuple

import jax
import jax.numpy as jnp
import numpy as np
from jax import lax
from jax.experimental import pallas as pl
from jax.experimental.pallas import tpu as pltpu

F32 = jnp.float32
BF16 = jnp.bfloat16

D_MODEL = 1024
POOL_WINDOWS = (2, 4, 8, 16)
POOL_GROUPS = len(POOL_WINDOWS)
POOL_WIDTH = D_MODEL // 2
POOL_GROUP_DIM = POOL_WIDTH // POOL_GROUPS
POOL_HALO = max(POOL_WINDOWS)
DA_HEADS = 8
DA_QK_DIM = 64
DA_V_DIM = 2 * DA_QK_DIM
DA_WIDTH = DA_HEADS * DA_V_DIM
XA_HEADS = 4
XA_HEAD_DIM = D_MODEL // XA_HEADS
PEER_HEADS = 8
PEER_N_KEYS = 128
PEER_N_EXPERTS = PEER_N_KEYS * PEER_N_KEYS
PEER_QUERY_DIM = 128
PEER_HALF = PEER_QUERY_DIM // 2
PEER_TOPK = 16
DEPTH = 1
DN_ALPHA = (2 * DEPTH) ** 0.25
LN_EPS = 1e-5
NEG_INF = -1e30
SQRT_HALF = math.sqrt(0.5)
LOG2E = float(np.float32(math.log2(math.e)))

VMEM_LIMIT_BYTES = 52 * 1024 * 1024
BF16_SUBLANES = 16


class Tiles(NamedTuple):
    attn: int = 512
    attn_heads: int = 2
    mix: int = 512
    mem_cols: int = 512
    peer_tokens: int = 512
    route_heads: int = 4
    peer_experts: int = 2048
    peer_chunk: int = 1024


TILES = Tiles()

_NT = (((1,), (1,)), ((), ()))


def _params(*sem):
    return pltpu.CompilerParams(dimension_semantics=sem, vmem_limit_bytes=VMEM_LIMIT_BYTES)


def _dot(a, b):
    return jnp.dot(a, b, preferred_element_type=F32)


def _dot_nt(a, b):
    return lax.dot_general(a, b, _NT, preferred_element_type=F32)


def _layer_norm(z, g, b):
    mu = jnp.mean(z, axis=-1, keepdims=True)
    zc = z - mu
    var = jnp.mean(zc * zc, axis=-1, keepdims=True)
    return zc * lax.rsqrt(var + LN_EPS) * g + b


def _proj_kernel(a_ref, w_ref, o_ref):
    o_ref[...] = _dot(a_ref[...].astype(BF16), w_ref[...]).astype(o_ref.dtype)


def _proj(a, w, tn, name):
    m, k = a.shape
    n = w.shape[1]
    return pl.pallas_call(
        _proj_kernel,
        grid=(n // tn,),
        in_specs=[pl.BlockSpec((m, k), lambda j: (0, 0)), pl.BlockSpec((k, tn), lambda j: (0, j))],
        out_specs=pl.BlockSpec((m, tn), lambda j: (0, j)),
        out_shape=jax.ShapeDtypeStruct((m, n), BF16),
        compiler_params=_params("parallel"),
        name=name,
    )(a, w)


def _proj_qkv_kernel(x_ref, wqk_ref, scale_ref, wvt_ref, qk_ref, vt_ref):
    xb = x_ref[...].astype(BF16)
    qk_ref[...] = (_dot(xb, wqk_ref[...]) * scale_ref[...]).astype(qk_ref.dtype)
    vt_ref[0] = _dot_nt(wvt_ref[...], xb).astype(vt_ref.dtype)


def _proj_qkv(x2d, w_qk, qk_scale, w_v_t, tk):
    m, k = x2d.shape
    n = w_qk.shape[1]
    nv = w_v_t.shape[0]
    return pl.pallas_call(
        _proj_qkv_kernel,
        grid=(m // tk,),
        in_specs=[
            pl.BlockSpec((tk, k), lambda j: (j, 0)),
            pl.BlockSpec((k, n), lambda j: (0, 0)),
            pl.BlockSpec((1, n), lambda j: (0, 0)),
            pl.BlockSpec((nv, k), lambda j: (0, 0)),
        ],
        out_specs=[pl.BlockSpec((tk, n), lambda j: (j, 0)), pl.BlockSpec((1, nv, tk), lambda j: (j, 0, 0))],
        out_shape=[jax.ShapeDtypeStruct((m, n), BF16), jax.ShapeDtypeStruct((m // tk, nv, tk), BF16)],
        compiler_params=_params("parallel"),
        name="proj_qkv",
    )(x2d, w_qk, qk_scale, w_v_t)


ATTN_ONES_ROWS = 16
ATTN_SKIP_LOG2 = 160.0
ALIBI_PIECES = 5


def _alibi_tables(t):
    slopes = np.exp2(-8.0 * np.arange(1, DA_HEADS + 1, dtype=np.float64) / DA_HEADS)
    ramp = float(LOG2E) * slopes[:, None] * np.arange(t, dtype=np.float64)[None, :]

    def pieces(v):
        out = []
        for _ in range(ALIBI_PIECES):
            piece = v.astype(np.float32).astype(BF16).astype(np.float64)
            out.append(piece)
            v = v - piece
        assert not v.any()
        return out

    ones = np.ones_like(ramp)
    pad = [np.zeros_like(ramp)] * (DA_V_DIM - 2 * ALIBI_PIECES)
    key_side = np.stack(pieces(ramp) + [ones] * ALIBI_PIECES + pad, axis=-1)
    query_side = np.stack([ones] * ALIBI_PIECES + pieces(-ramp) + pad, axis=-1)
    return jnp.asarray(key_side, BF16), jnp.asarray(query_side, BF16)


def _attn_kernel(slopes_ref, inv_tile_ref, lq1_ref, lk1_ref, lq2_ref, lk2_ref, sw_ref, q_ref, qb_ref, k_ref, kb_ref,
                 vt_ref, causal_ref, o_ref, sa_ref, sb_ref, m_ref, acc_ref, kmax_ref, *, t, heads, lam_init):
    hg = pl.program_id(1)
    i = pl.program_id(2)
    slope_log2 = [slopes_ref[hg * heads + g] * LOG2E for g in range(heads)]

    lane = lax.broadcasted_iota(jnp.int32, (t, DA_V_DIM), 1)
    q_maps = []
    for g in range(heads):
        q = q_ref[:, g * DA_V_DIM:(g + 1) * DA_V_DIM]
        zero = jnp.zeros_like(q)
        q_maps += [jnp.concatenate([jnp.where(lane < DA_QK_DIM, q, zero), qb_ref[g]], axis=1),
                   jnp.concatenate([jnp.where(lane >= DA_QK_DIM, q, zero), qb_ref[g]], axis=1)]
    ones_rows = jnp.ones((ATTN_ONES_ROWS, t), BF16)

    m_ref[...] = jnp.full(m_ref.shape, NEG_INF, F32)
    acc_ref[...] = jnp.zeros(acc_ref.shape, F32)

    def scores(j, s_ref):
        rows = pl.ds(pl.multiple_of(j * t, t), t)
        for g in range(heads):
            k_aug = jnp.concatenate([k_ref[rows, g * DA_V_DIM:(g + 1) * DA_V_DIM], kb_ref[g]], axis=1)
            for mp in range(2):
                s_ref[2 * g + mp] = _dot_nt(k_aug, q_maps[2 * g + mp])

    def consume(j, s_ref, diagonal=False):
        for g in range(heads):
            vt_aug = jnp.concatenate([vt_ref[j, g * DA_V_DIM:(g + 1) * DA_V_DIM, :], ones_rows], axis=0)
            shift = slope_log2[g] * ((j - i) * t).astype(F32)
            for mp in range(2):
                c = 2 * g + mp
                s = s_ref[c]
                if diagonal:
                    s = s + causal_ref[...]
                m_old = m_ref[c]
                m_new = jnp.maximum(m_old, jnp.max(s, axis=0, keepdims=True) + shift)
                p = jnp.exp2(s - (m_new - shift))
                acc_ref[c] = jnp.exp2(m_old - m_new) * acc_ref[c] + _dot(vt_aug, p.astype(BF16))
                m_ref[c] = m_new

    def largest_row_norm(rows):
        sq = jnp.square(rows.astype(F32))
        return jnp.max(jnp.sqrt(jnp.max(jnp.sum(sq, axis=1, keepdims=True), axis=0, keepdims=True)))

    @pl.when(i == 0)
    def _():
        for g in range(heads):
            kmax_ref[g] = largest_row_norm(k_ref[:, g * DA_V_DIM:(g + 1) * DA_V_DIM])

    scores(i, sa_ref)
    scores(jnp.maximum(i - 1, 0), sb_ref)
    consume(i, sa_ref, diagonal=True)

    keep = jnp.int32(0)
    for g in range(heads):
        m_min = jnp.min(m_ref[2 * g:2 * g + 2])
        q_max = largest_row_norm(q_ref[:, g * DA_V_DIM:(g + 1) * DA_V_DIM])
        reach = (ATTN_SKIP_LOG2 + q_max * kmax_ref[g] - m_min) * inv_tile_ref[hg * heads + g] + (t - 1) / t
        keep = jnp.maximum(keep, jnp.minimum(reach, float(2 ** 20)).astype(jnp.int32))
    keep = jnp.clip(keep, 0, i)

    def tile_pair(jj, carry):
        dist = 1 + 2 * jj
        scores(jnp.maximum(i - dist - 1, 0), sa_ref)
        consume(i - dist, sb_ref)
        scores(jnp.maximum(i - dist - 2, 0), sb_ref)
        consume(i - dist - 1, sa_ref)
        return carry

    lax.fori_loop(0, keep // 2, tile_pair, 0)

    @pl.when(keep % 2 == 1)
    def _():
        consume(i - keep, sb_ref)

    lam = (jnp.exp(jnp.sum(lq1_ref[...] * lk1_ref[...], axis=-1, keepdims=True))
           - jnp.exp(jnp.sum(lq2_ref[...] * lk2_ref[...], axis=-1, keepdims=True)) + lam_init)
    for g in range(heads):
        a1 = acc_ref[2 * g]
        a2 = acc_ref[2 * g + 1]
        o_t = (a1[:DA_V_DIM] / a1[DA_V_DIM:DA_V_DIM + 1]
               - lam * (a2[:DA_V_DIM] / a2[DA_V_DIM:DA_V_DIM + 1]))
        o = o_t.T
        o = o * lax.rsqrt(jnp.mean(o * o, axis=-1, keepdims=True) + LN_EPS)
        o_ref[:, g * DA_V_DIM:(g + 1) * DA_V_DIM] = (o * sw_ref[...] * (1.0 - lam_init)).astype(o_ref.dtype)


def _diff_attention(qk, vtb, slopes, lq1, lk1, lq2, lk2, subln_w, batch, seq, lam_init, t, heads):
    n_tok = qk.shape[0]
    nq = seq // t
    width = heads * DA_V_DIM
    key_bias, query_bias = _alibi_tables(t)
    pos = np.arange(t)
    causal = jnp.asarray(np.where(pos[:, None] <= pos[None, :], 0.0, NEG_INF), F32)
    vec = pl.BlockSpec((1, DA_QK_DIM), lambda b, h, i: (0, 0))
    bias_spec = pl.BlockSpec((heads, t, DA_V_DIM), lambda b, h, i: (h, 0, 0))
    maps = 2 * heads
    return pl.pallas_call(
        functools.partial(_attn_kernel, t=t, heads=heads, lam_init=lam_init),
        grid=(batch, DA_HEADS // heads, nq),
        in_specs=[
            pl.BlockSpec(memory_space=pltpu.SMEM),
            pl.BlockSpec(memory_space=pltpu.SMEM),
            vec, vec, vec, vec,
            pl.BlockSpec((1, DA_V_DIM), lambda b, h, i: (0, 0)),
            pl.BlockSpec((t, width), lambda b, h, i: (b * nq + i, h)),
            bias_spec,
            pl.BlockSpec((seq, width), lambda b, h, i: (b, DA_HEADS // heads + h)),
            bias_spec,
            pl.BlockSpec((nq, width, t), lambda b, h, i: (b, h, 0)),
            pl.BlockSpec((t, t), lambda b, h, i: (0, 0)),
        ],
        out_specs=pl.BlockSpec((t, width), lambda b, h, i: (b * nq + i, h)),
        out_shape=jax.ShapeDtypeStruct((n_tok, DA_WIDTH), BF16),
        scratch_shapes=[
            pltpu.VMEM((maps, t, t), F32),
            pltpu.VMEM((maps, t, t), F32),
            pltpu.VMEM((maps, 1, t), F32),
            pltpu.VMEM((maps, DA_V_DIM + ATTN_ONES_ROWS, t), F32),
            pltpu.SMEM((heads,), F32),
        ],
        compiler_params=_params("parallel", "parallel", "arbitrary"),
        name="diff_attention",
    )(slopes, 1.0 / (slopes * (LOG2E * t)), lq1, lk1, lq2, lk2, subln_w, qk, query_bias, qk, key_bias, vtb, causal)


def _mix_kernel(x_ref, xh_ref, o_ref, kv_ref, wp_ref, wgp_ref, wga_ref, poolw_ref, pscale_ref, wbp_ref, wba_ref,
                wout_ref, g_ref, b_ref, wq_ref, wo_ref, g2_ref, b2_ref, out_ref, e_ref, *, tm, seq):
    i = pl.program_id(0)
    start = (i * tm) % seq
    x = x_ref[...]
    xb = x.astype(BF16)

    p = _dot(xb, wp_ref[...])
    p_halo = _dot(xh_ref[...].astype(BF16), wp_ref[...])
    e_ref[0:POOL_HALO, :] = jnp.where(start == 0, 0.0, p_halo)
    e_ref[POOL_HALO:, :] = p

    pos = (start + 1 + lax.broadcasted_iota(jnp.int32, (tm, 1), 0)).astype(F32)
    ys = []
    for g, w in enumerate(POOL_WINDOWS):
        cols = slice(g * POOL_GROUP_DIM, (g + 1) * POOL_GROUP_DIM)
        acc = e_ref[POOL_HALO:POOL_HALO + tm, cols]
        for back in range(1, w):
            acc = acc + e_ref[POOL_HALO - back:POOL_HALO - back + tm, cols]
        pooled = acc / jnp.minimum(pos, float(w)) - e_ref[POOL_HALO:POOL_HALO + tm, cols]
        ys.append(_dot(pooled.astype(BF16), poolw_ref[g]))
    y = jnp.concatenate(ys, axis=-1) * pscale_ref[...]
    y_pool = _dot(y.astype(BF16), wbp_ref[...])

    gate_p = _dot(xb, wgp_ref[...])
    gate_a = _dot(xb, wga_ref[...])
    y_attn = _dot(o_ref[...], wba_ref[...])
    merged = jax.nn.sigmoid(gate_p) * y_pool + jax.nn.sigmoid(gate_a) * y_attn
    z = DN_ALPHA * x + _dot(merged.astype(BF16), wout_ref[...])
    x1 = _layer_norm(z, g_ref[...], b_ref[...])
    out_ref[...] = _xattn(x1, kv_ref, wq_ref, wo_ref, g2_ref, b2_ref)


def _mix_xattn(x2d, o, kvm, w_p, w_gp, w_ga, pool_w, pool_scale, w_bp, w_ba, w_out, g1, b1, w_cq, w_co, g2, b2, seq, tm):
    n_tok = x2d.shape[0]
    n_mem = kvm.shape[1]
    halo_blocks = tm // POOL_HALO

    def full(a):
        return pl.BlockSpec(a.shape, lambda i: (0,) * a.ndim)

    weights = (w_p, w_gp, w_ga, pool_w, pool_scale, w_bp, w_ba, w_out, g1, b1, w_cq, w_co, g2, b2)
    return pl.pallas_call(
        functools.partial(_mix_kernel, tm=tm, seq=seq),
        grid=(n_tok // tm,),
        in_specs=[
            pl.BlockSpec((tm, D_MODEL), lambda i: (i, 0)),
            pl.BlockSpec((POOL_HALO, D_MODEL), lambda i: (jnp.maximum(i * halo_blocks - 1, 0), 0)),
            pl.BlockSpec((tm, DA_WIDTH), lambda i: (i, 0)),
            pl.BlockSpec((1, n_mem, 2 * D_MODEL), lambda i: ((i * tm) // seq, 0, 0)),
        ] + [full(w) for w in weights],
        out_specs=pl.BlockSpec((tm, D_MODEL), lambda i: (i, 0)),
        out_shape=jax.ShapeDtypeStruct((n_tok, D_MODEL), F32),
        scratch_shapes=[pltpu.VMEM((tm + POOL_HALO, POOL_WIDTH), F32)],
        compiler_params=_params("parallel"),
        name="mix_xattn",
    )(x2d, x2d, o, kvm, *weights)


def _xattn(x, kv_ref, wq_ref, wo_ref, g_ref, b_ref):
    q = (_dot(x.astype(BF16), wq_ref[...]) * (XA_HEAD_DIM ** -0.5)).astype(BF16)
    kv = kv_ref[0]
    outs = []
    for h in range(XA_HEADS):
        cols = slice(h * XA_HEAD_DIM, (h + 1) * XA_HEAD_DIM)
        s = _dot_nt(q[:, cols], kv[:, cols])
        p = jnp.exp(s - jnp.max(s, axis=-1, keepdims=True))
        l = jnp.sum(p, axis=-1, keepdims=True)
        v = kv[:, D_MODEL + h * XA_HEAD_DIM:D_MODEL + (h + 1) * XA_HEAD_DIM]
        outs.append(_dot(p.astype(BF16), v) / l)
    o = jnp.concatenate(outs, axis=-1)
    z = DN_ALPHA * x + _dot(o.astype(BF16), wo_ref[...])
    return _layer_norm(z, g_ref[...], b_ref[...])


SUBLANES = 8


def _sort_network(n):
    def merge(lo, hi, r):
        step = 2 * r
        if step < hi - lo:
            yield from merge(lo, hi, step)
            yield from merge(lo + r, hi, step)
            yield from ((i, i + r) for i in range(lo + r, hi - r, step))
        else:
            yield (lo, lo + r)

    def sort(lo, hi):
        if hi - lo >= 1:
            mid = lo + (hi - lo) // 2
            yield from sort(lo, mid)
            yield from sort(mid + 1, hi)
            yield from merge(lo, hi, 1)

    return tuple(sort(0, n - 1))


def _exchange(v, i, j):
    if v[j] is None:
        return
    if v[i] is None:
        v[i], v[j] = v[j], None
        return
    v[i], v[j] = jnp.maximum(v[i], v[j]), jnp.minimum(v[i], v[j])


def _top_sorted(blocks, k):
    v = list(blocks) + [None] * (k - len(blocks))
    for i, j in _sort_network(k):
        _exchange(v, i, j)
    shift = SUBLANES // 2
    while shift >= 1:
        other = [None if a is None else pltpu.roll(a, shift, 0) for a in v]
        v = [b if a is None else (a if b is None else jnp.maximum(a, b)) for a, b in zip(v, reversed(other))]
        stride = k // 2
        while stride >= 1:
            for i in range(k):
                if not i & stride:
                    _exchange(v, i, i + stride)
            stride //= 2
        shift //= 2
    return v


def _route_kernel(x_ref, wq_ref, keys_ref, rank2_ref, e2_ref, count_ref, d_ref, *, heads):
    q_t = _dot_nt(wq_ref[...], x_ref[...].astype(BF16)).astype(BF16)
    for g in range(heads):
        q_g = q_t[g * PEER_QUERY_DIM:(g + 1) * PEER_QUERY_DIM]
        s1 = _dot(keys_ref[g, 0], q_g[:PEER_HALF])
        s2 = _dot(keys_ref[g, 1], q_g[PEER_HALF:])
        _route_head(s1, s2, g, rank2_ref, e2_ref, count_ref, d_ref)


def _route_head(s1, s2, g, rank2_ref, e2_ref, count_ref, d_ref):
    k = PEER_TOPK
    n_blocks = PEER_N_KEYS // SUBLANES
    s1_blocks = [s1[r * SUBLANES:(r + 1) * SUBLANES] for r in range(n_blocks)]
    s2_blocks = [s2[r * SUBLANES:(r + 1) * SUBLANES] for r in range(n_blocks)]
    v1 = _top_sorted(s1_blocks, k)
    v2 = _top_sorted(s2_blocks, k)

    row = lax.broadcasted_iota(jnp.int32, v1[0].shape, 0)

    def pack(vals):
        out = vals[0]
        for r in range(1, SUBLANES):
            out = jnp.where(row == r, vals[r], out)
        return out

    v1_lo, v1_hi, v2_hi = pack(v1[:SUBLANES]), pack(v1[SUBLANES:]), pack(v2[SUBLANES:])
    cand = ([v1_lo + v2[0], v1_hi + v2[0]] + [v1_lo + v2[b] for b in range(1, SUBLANES)] + [v2_hi + v1[0]])
    top = _top_sorted(cand, k)
    tau = top[k - 1]
    z = jnp.ones(tau.shape, F32)
    for c in top[1:]:
        z = z + jnp.exp(c - top[0])
    inv_z = 1.0 / z

    partners = []
    for a in range(k):
        n = jnp.zeros(tau.shape, F32)
        for b in range(k // (a + 1)):
            n = jnp.where(v1[a] + v2[b] >= tau, float(b + 1), n)
        partners.append(n)

    rank2, e2, count, d = [], [], [], []
    for blk1, blk2 in zip(s1_blocks, s2_blocks):
        r2 = jnp.full(blk2.shape, float(k), F32)
        cnt = jnp.zeros(blk1.shape, F32)
        for b in reversed(range(k)):
            r2 = jnp.where(blk2 >= v2[b], float(b), r2)
            cnt = jnp.where(blk1 >= v1[b], partners[b], cnt)
        rank2.append(r2)
        count.append(cnt)
        e2.append(jnp.exp(blk2 - v2[0]))
        d.append(jnp.exp(blk1 - v1[0]) * inv_z)
    rank2_ref[g] = jnp.concatenate(rank2, axis=0).astype(BF16)
    e2_ref[g] = jnp.concatenate(e2, axis=0).astype(BF16)
    count_ref[g] = jnp.concatenate(count, axis=0)
    d_ref[g] = jnp.concatenate(d, axis=0)


def _peer_route(x2, wq_t, keys, tm, heads):
    n_tok = x2.shape[0]
    out_spec = pl.BlockSpec((heads, PEER_N_KEYS, tm), lambda i, h: (h, 0, i))
    out_shape = [jax.ShapeDtypeStruct((PEER_HEADS, PEER_N_KEYS, n_tok), dt) for dt in (BF16, BF16, F32, F32)]
    return pl.pallas_call(
        functools.partial(_route_kernel, heads=heads),
        grid=(n_tok // tm, PEER_HEADS // heads),
        in_specs=[
            pl.BlockSpec((tm, D_MODEL), lambda i, h: (i, 0)),
            pl.BlockSpec((heads * PEER_QUERY_DIM, D_MODEL), lambda i, h: (h, 0)),
            pl.BlockSpec((heads, 2, PEER_N_KEYS, PEER_HALF), lambda i, h: (h, 0, 0, 0)),
        ],
        out_specs=[out_spec] * 4,
        out_shape=out_shape,
        compiler_params=_params("parallel", "parallel"),
        name="peer_route",
    )(x2, wq_t, keys)


def _peer_kernel(x_ref, u_ref, vt_ref, rank2_ref, e2_ref, count_ref, d_ref, g_ref, b_ref, out_ref,
                 xb_ref, acc_ref, a_ref, *, chunks, rows_per_chunk):
    e = pl.program_id(1)

    @pl.when(e == 0)
    def _():
        xb_ref[...] = (x_ref[...] * SQRT_HALF).astype(BF16)
        acc_ref[...] = jnp.zeros(acc_ref.shape, F32)

    xb = xb_ref[...]
    chunk = rows_per_chunk * PEER_N_KEYS
    tb = xb.shape[0]

    def sublane_rows(ref, h, row):
        tile = jnp.broadcast_to(ref[h, row:row + 1, :], (BF16_SUBLANES, tb)).astype(BF16)
        return jnp.concatenate([tile] * (PEER_N_KEYS // BF16_SUBLANES), axis=0)

    def activations(c):
        return _dot_nt(u_ref[c * chunk:(c + 1) * chunk, :], xb)

    a_ref[0] = activations(0)
    for c in range(chunks):
        if c + 1 < chunks:
            a_ref[(c + 1) % 2] = activations(c + 1)
        parts = []
        for r in range(rows_per_chunk):
            row = c * rows_per_chunk + r
            z = a_ref[c % 2, r * PEER_N_KEYS:(r + 1) * PEER_N_KEYS, :]
            act = z + z * lax.erf(z)
            gate = None
            for h in range(PEER_HEADS):
                cnt = sublane_rows(count_ref, h, row)
                d = sublane_rows(d_ref, h, row)
                term = jnp.where(rank2_ref[h] < cnt, e2_ref[h], jnp.zeros((), BF16)) * d
                gate = term if gate is None else gate + term
            parts.append(gate * act.astype(BF16))
        w = jnp.concatenate(parts, axis=0)
        acc_ref[...] += _dot(vt_ref[:, c * chunk:(c + 1) * chunk], w)

    @pl.when(e == pl.num_programs(1) - 1)
    def _():
        z = DN_ALPHA * x_ref[...] + SQRT_HALF * acc_ref[...].T
        out_ref[...] = _layer_norm(z, g_ref[...], b_ref[...])


def _peer_ln3(x2, u, vt, rank2, e2, count, d, g, b, tb, te, chunk):
    n_tok = x2.shape[0]
    rows_per_tile = te // PEER_N_KEYS
    key_spec = pl.BlockSpec((PEER_HEADS, PEER_N_KEYS, tb), lambda t, e: (0, 0, t))
    row_spec = pl.BlockSpec((PEER_HEADS, rows_per_tile, tb), lambda t, e: (0, e, t))
    vec = pl.BlockSpec((1, D_MODEL), lambda t, e: (0, 0))
    return pl.pallas_call(
        functools.partial(_peer_kernel, chunks=te // chunk, rows_per_chunk=chunk // PEER_N_KEYS),
        grid=(n_tok // tb, PEER_N_EXPERTS // te),
        in_specs=[
            pl.BlockSpec((tb, D_MODEL), lambda t, e: (t, 0)),
            pl.BlockSpec((te, D_MODEL), lambda t, e: (e, 0)),
            pl.BlockSpec((D_MODEL, te), lambda t, e: (0, e)),
            key_spec, key_spec, row_spec, row_spec,
            vec, vec,
        ],
        out_specs=pl.BlockSpec((tb, D_MODEL), lambda t, e: (t, 0)),
        out_shape=jax.ShapeDtypeStruct((n_tok, D_MODEL), F32),
        scratch_shapes=[
            pltpu.VMEM((tb, D_MODEL), BF16),
            pltpu.VMEM((D_MODEL, tb), F32),
            pltpu.VMEM((2, chunk, tb), F32),
        ],
        compiler_params=_params("parallel", "arbitrary"),
        name="peer_dense",
    )(x2, u, vt, rank2, e2, count, d, g, b)


def kernel(x, mem, w_in, pool_w, pool_scale, w_br_pool, lambda_q1, lambda_k1, lambda_q2, lambda_k2, subln_w, w_br_attn, w_out, ln1_g, ln1_b, w_cq, w_ckv, w_co, ln2_g, ln2_b, w_pq, sub_keys, expert_u, expert_v, ln3_g, ln3_b):
    batch, seq, _ = x.shape
    n_tok = batch * seq
    n_mem = mem.shape[1]
    tiles = TILES
    assert seq % tiles.attn == 0 and seq % tiles.mix == 0 and n_tok % tiles.peer_tokens == 0
    slopes = jnp.exp2(-8.0 * jnp.arange(1, DA_HEADS + 1, dtype=F32) / DA_HEADS)

    def row(v):
        return v.reshape(1, -1).astype(F32)

    for l in range(DEPTH):
        lam_init = 0.8 - 0.6 * math.exp(-0.3 * l)
        x2d = x.reshape(n_tok, D_MODEL)
        w_in_b = w_in[l].astype(BF16)
        q0 = POOL_WIDTH
        k0 = q0 + DA_WIDTH
        v0 = k0 + DA_WIDTH
        gp0 = v0 + DA_WIDTH
        ga0 = gp0 + D_MODEL

        qk_scale = jnp.concatenate([jnp.full((1, DA_WIDTH), LOG2E * DA_QK_DIM ** -0.5, F32),
                                    jnp.ones((1, DA_WIDTH), F32)], axis=1)
        qk, vtb = _proj_qkv(x2d, w_in_b[:, q0:v0], qk_scale, w_in_b[:, v0:gp0].T, tiles.attn)
        o = _diff_attention(qk, vtb, slopes, row(lambda_q1[l]), row(lambda_k1[l]), row(lambda_q2[l]),
                            row(lambda_k2[l]), row(subln_w[l]), batch, seq, lam_init, tiles.attn, tiles.attn_heads)

        kvm = _proj(mem.reshape(batch * n_mem, D_MODEL), w_ckv[l].astype(BF16), tiles.mem_cols,
                    "proj_mem_kv").reshape(batch, n_mem, 2 * D_MODEL)
        x2 = _mix_xattn(x2d, o, kvm, w_in_b[:, :POOL_WIDTH], w_in_b[:, gp0:ga0], w_in_b[:, ga0:],
                        pool_w[l].astype(BF16), row(pool_scale[l]), w_br_pool[l].astype(BF16),
                        w_br_attn[l].astype(BF16), w_out[l].astype(BF16), row(ln1_g[l]), row(ln1_b[l]),
                        w_cq[l].astype(BF16), w_co[l].astype(BF16), row(ln2_g[l]), row(ln2_b[l]), seq, tiles.mix)

        rank2, e2, count, d = _peer_route(x2, w_pq[l].T.astype(BF16), sub_keys[l].astype(BF16),
                                          tiles.peer_tokens, tiles.route_heads)
        x3 = _peer_ln3(x2, expert_u[l].astype(BF16), expert_v[l].T.astype(BF16), rank2, e2, count, d,
                       row(ln3_g[l]), row(ln3_b[l]), tiles.peer_tokens, tiles.peer_experts, tiles.peer_chunk)
        x = x3.reshape(batch, seq, D_MODEL)
    return x
```

```python
import functools
import math
from typing import NamedTuple

import jax
import jax.numpy as jnp
import numpy as np
from jax import lax
from jax.experimental import pallas as pl
from jax.experimental.pallas import tpu as pltpu

F32 = jnp.float32
BF16 = jnp.bfloat16

D_MODEL = 1024
POOL_WINDOWS = (2, 4, 8, 16)
POOL_GROUPS = len(POOL_WINDOWS)
POOL_WIDTH = D_MODEL // 2
POOL_GROUP_DIM = POOL_WIDTH // POOL_GROUPS
POOL_HALO = max(POOL_WINDOWS)
DA_HEADS = 8
DA_QK_DIM = 64
DA_V_DIM = 2 * DA_QK_DIM
DA_WIDTH = DA_HEADS * DA_V_DIM
XA_HEADS = 4
XA_HEAD_DIM = D_MODEL // XA_HEADS
PEER_HEADS = 8
PEER_N_KEYS = 128
PEER_N_EXPERTS = PEER_N_KEYS * PEER_N_KEYS
PEER_QUERY_DIM = 128
PEER_HALF = PEER_QUERY_DIM // 2
PEER_TOPK = 16
DEPTH = 1
DN_ALPHA = (2 * DEPTH) ** 0.25
LN_EPS = 1e-5
NEG_INF = -1e30
SQRT_HALF = math.sqrt(0.5)
LOG2E = float(np.float32(math.log2(math.e)))

VMEM_LIMIT_BYTES = 52 * 1024 * 1024
BF16_SUBLANES = 16


class Tiles(NamedTuple):
    attn: int = 512
    attn_heads: int = 2
    mix: int = 512
    mem_cols: int = 512
    peer_tokens: int = 512
    route_heads: int = 8
    peer_experts: int = 2048
    peer_chunk: int = 1024


TILES = Tiles()

_NT = (((1,), (1,)), ((), ()))


def _params(*sem):
    return pltpu.CompilerParams(dimension_semantics=sem, vmem_limit_bytes=VMEM_LIMIT_BYTES)


def _dot(a, b):
    return jnp.dot(a, b, preferred_element_type=F32)


def _dot_nt(a, b):
    return lax.dot_general(a, b, _NT, preferred_element_type=F32)


def _layer_norm(z, g, b):
    mu = jnp.mean(z, axis=-1, keepdims=True)
    zc = z - mu
    var = jnp.mean(zc * zc, axis=-1, keepdims=True)
    return zc * lax.rsqrt(var + LN_EPS) * g + b


def _proj_kernel(a_ref, w_ref, o_ref):
    o_ref[...] = _dot(a_ref[...].astype(BF16), w_ref[...]).astype(o_ref.dtype)


def _proj(a, w, tn, name):
    m, k = a.shape
    n = w.shape[1]
    return pl.pallas_call(
        _proj_kernel,
        grid=(n // tn,),
        in_specs=[pl.BlockSpec((m, k), lambda j: (0, 0)), pl.BlockSpec((k, tn), lambda j: (0, j))],
        out_specs=pl.BlockSpec((m, tn), lambda j: (0, j)),
        out_shape=jax.ShapeDtypeStruct((m, n), BF16),
        compiler_params=_params("parallel"),
        name=name,
    )(a, w)


def _proj_qkv_kernel(x_ref, wqk_ref, scale_ref, wvt_ref, qk_ref, vt_ref):
    xb = x_ref[...].astype(BF16)
    qk_ref[...] = (_dot(xb, wqk_ref[...]) * scale_ref[...]).astype(qk_ref.dtype)
    vt_ref[0] = _dot_nt(wvt_ref[...], xb).astype(vt_ref.dtype)


def _proj_qkv(x2d, w_qk, qk_scale, w_v_t, tk):
    m, k = x2d.shape
    n = w_qk.shape[1]
    nv = w_v_t.shape[0]
    return pl.pallas_call(
        _proj_qkv_kernel,
        grid=(m // tk,),
        in_specs=[
            pl.BlockSpec((tk, k), lambda j: (j, 0)),
            pl.BlockSpec((k, n), lambda j: (0, 0)),
            pl.BlockSpec((1, n), lambda j: (0, 0)),
            pl.BlockSpec((nv, k), lambda j: (0, 0)),
        ],
        out_specs=[pl.BlockSpec((tk, n), lambda j: (j, 0)), pl.BlockSpec((1, nv, tk), lambda j: (j, 0, 0))],
        out_shape=[jax.ShapeDtypeStruct((m, n), BF16), jax.ShapeDtypeStruct((m // tk, nv, tk), BF16)],
        compiler_params=_params("parallel"),
        name="proj_qkv",
    )(x2d, w_qk, qk_scale, w_v_t)


ATTN_ONES_ROWS = 16
ATTN_SKIP_LOG2 = 160.0
ALIBI_PIECES = 5


def _alibi_tables(t):
    slopes = np.exp2(-8.0 * np.arange(1, DA_HEADS + 1, dtype=np.float64) / DA_HEADS)
    ramp = float(LOG2E) * slopes[:, None] * np.arange(t, dtype=np.float64)[None, :]

    def pieces(v):
        out = []
        for _ in range(ALIBI_PIECES):
            piece = v.astype(np.float32).astype(BF16).astype(np.float64)
            out.append(piece)
            v = v - piece
        assert not v.any()
        return out

    ones = np.ones_like(ramp)
    pad = [np.zeros_like(ramp)] * (DA_V_DIM - 2 * ALIBI_PIECES)
    key_side = np.stack(pieces(ramp) + [ones] * ALIBI_PIECES + pad, axis=-1)
    query_side = np.stack([ones] * ALIBI_PIECES + pieces(-ramp) + pad, axis=-1)
    return jnp.asarray(key_side, BF16), jnp.asarray(query_side, BF16)


def _attn_kernel(slopes_ref, inv_tile_ref, lq1_ref, lk1_ref, lq2_ref, lk2_ref, sw_ref, q_ref, qb_ref, k_ref, kb_ref,
                 vt_ref, causal_ref, o_ref, sa_ref, sb_ref, m_ref, acc_ref, kmax_ref, *, t, heads, lam_init):
    hg = pl.program_id(1)
    i = pl.program_id(2)
    slope_log2 = [slopes_ref[hg * heads + g] * LOG2E for g in range(heads)]

    lane = lax.broadcasted_iota(jnp.int32, (t, DA_V_DIM), 1)
    q_maps = []
    for g in range(heads):
        q = q_ref[:, g * DA_V_DIM:(g + 1) * DA_V_DIM]
        zero = jnp.zeros_like(q)
        q_maps += [jnp.concatenate([jnp.where(lane < DA_QK_DIM, q, zero), qb_ref[g]], axis=1),
                   jnp.concatenate([jnp.where(lane >= DA_QK_DIM, q, zero), qb_ref[g]], axis=1)]
    ones_rows = jnp.ones((ATTN_ONES_ROWS, t), BF16)

    m_ref[...] = jnp.full(m_ref.shape, NEG_INF, F32)
    acc_ref[...] = jnp.zeros(acc_ref.shape, F32)

    def scores(j, s_ref):
        rows = pl.ds(pl.multiple_of(j * t, t), t)
        for g in range(heads):
            k_aug = jnp.concatenate([k_ref[rows, g * DA_V_DIM:(g + 1) * DA_V_DIM], kb_ref[g]], axis=1)
            for mp in range(2):
                s_ref[2 * g + mp] = _dot_nt(k_aug, q_maps[2 * g + mp])

    def consume(j, s_ref, diagonal=False):
        for g in range(heads):
            vt_aug = jnp.concatenate([vt_ref[j, g * DA_V_DIM:(g + 1) * DA_V_DIM, :], ones_rows], axis=0)
            shift = slope_log2[g] * ((j - i) * t).astype(F32)
            for mp in range(2):
                c = 2 * g + mp
                s = s_ref[c]
                if diagonal:
                    s = s + causal_ref[...]
                m_old = m_ref[c]
                m_new = jnp.maximum(m_old, jnp.max(s, axis=0, keepdims=True) + shift)
                p = jnp.exp2(s - (m_new - shift))
                acc_ref[c] = jnp.exp2(m_old - m_new) * acc_ref[c] + _dot(vt_aug, p.astype(BF16))
                m_ref[c] = m_new

    def largest_row_norm(rows):
        sq = jnp.square(rows.astype(F32))
        return jnp.max(jnp.sqrt(jnp.max(jnp.sum(sq, axis=1, keepdims=True), axis=0, keepdims=True)))

    @pl.when(i == 0)
    def _():
        for g in range(heads):
            kmax_ref[g] = largest_row_norm(k_ref[:, g * DA_V_DIM:(g + 1) * DA_V_DIM])

    scores(i, sa_ref)
    scores(jnp.maximum(i - 1, 0), sb_ref)
    consume(i, sa_ref, diagonal=True)

    keep = jnp.int32(0)
    for g in range(heads):
        m_min = jnp.min(m_ref[2 * g:2 * g + 2])
        q_max = largest_row_norm(q_ref[:, g * DA_V_DIM:(g + 1) * DA_V_DIM])
        reach = (ATTN_SKIP_LOG2 + q_max * kmax_ref[g] - m_min) * inv_tile_ref[hg * heads + g] + (t - 1) / t
        keep = jnp.maximum(keep, jnp.minimum(reach, float(2 ** 20)).astype(jnp.int32))
    keep = jnp.clip(keep, 0, i)

    def tile_pair(jj, carry):
        dist = 1 + 2 * jj
        scores(jnp.maximum(i - dist - 1, 0), sa_ref)
        consume(i - dist, sb_ref)
        scores(jnp.maximum(i - dist - 2, 0), sb_ref)
        consume(i - dist - 1, sa_ref)
        return carry

    lax.fori_loop(0, keep // 2, tile_pair, 0)

    @pl.when(keep % 2 == 1)
    def _():
        consume(i - keep, sb_ref)

    lam = (jnp.exp(jnp.sum(lq1_ref[...] * lk1_ref[...], axis=-1, keepdims=True))
           - jnp.exp(jnp.sum(lq2_ref[...] * lk2_ref[...], axis=-1, keepdims=True)) + lam_init)
    for g in range(heads):
        a1 = acc_ref[2 * g]
        a2 = acc_ref[2 * g + 1]
        o_t = (a1[:DA_V_DIM] / a1[DA_V_DIM:DA_V_DIM + 1]
               - lam * (a2[:DA_V_DIM] / a2[DA_V_DIM:DA_V_DIM + 1]))
        o = o_t.T
        o = o * lax.rsqrt(jnp.mean(o * o, axis=-1, keepdims=True) + LN_EPS)
        o_ref[:, g * DA_V_DIM:(g + 1) * DA_V_DIM] = (o * sw_ref[...] * (1.0 - lam_init)).astype(o_ref.dtype)


def _diff_attention(qk, vtb, slopes, lq1, lk1, lq2, lk2, subln_w, batch, seq, lam_init, t, heads):
    n_tok = qk.shape[0]
    nq = seq // t
    width = heads * DA_V_DIM
    key_bias, query_bias = _alibi_tables(t)
    pos = np.arange(t)
    causal = jnp.asarray(np.where(pos[:, None] <= pos[None, :], 0.0, NEG_INF), F32)
    vec = pl.BlockSpec((1, DA_QK_DIM), lambda b, h, i: (0, 0))
    bias_spec = pl.BlockSpec((heads, t, DA_V_DIM), lambda b, h, i: (h, 0, 0))
    maps = 2 * heads
    return pl.pallas_call(
        functools.partial(_attn_kernel, t=t, heads=heads, lam_init=lam_init),
        grid=(batch, DA_HEADS // heads, nq),
        in_specs=[
            pl.BlockSpec(memory_space=pltpu.SMEM),
            pl.BlockSpec(memory_space=pltpu.SMEM),
            vec, vec, vec, vec,
            pl.BlockSpec((1, DA_V_DIM), lambda b, h, i: (0, 0)),
            pl.BlockSpec((t, width), lambda b, h, i: (b * nq + i, h)),
            bias_spec,
            pl.BlockSpec((seq, width), lambda b, h, i: (b, DA_HEADS // heads + h)),
            bias_spec,
            pl.BlockSpec((nq, width, t), lambda b, h, i: (b, h, 0)),
            pl.BlockSpec((t, t), lambda b, h, i: (0, 0)),
        ],
        out_specs=pl.BlockSpec((t, width), lambda b, h, i: (b * nq + i, h)),
        out_shape=jax.ShapeDtypeStruct((n_tok, DA_WIDTH), BF16),
        scratch_shapes=[
            pltpu.VMEM((maps, t, t), F32),
            pltpu.VMEM((maps, t, t), F32),
            pltpu.VMEM((maps, 1, t), F32),
            pltpu.VMEM((maps, DA_V_DIM + ATTN_ONES_ROWS, t), F32),
            pltpu.SMEM((heads,), F32),
        ],
        compiler_params=_params("parallel", "parallel", "arbitrary"),
        name="diff_attention",
    )(slopes, 1.0 / (slopes * (LOG2E * t)), lq1, lk1, lq2, lk2, subln_w, qk, query_bias, qk, key_bias, vtb, causal)


def _mix_kernel(x_ref, xh_ref, o_ref, kv_ref, wp_ref, wgp_ref, wga_ref, poolw_ref, pscale_ref, wbp_ref, wba_ref,
                wout_ref, g_ref, b_ref, wq_ref, wo_ref, g2_ref, b2_ref, out_ref, e_ref, *, tm, seq):
    i = pl.program_id(0)
    start = (i * tm) % seq
    x = x_ref[...]
    xb = x.astype(BF16)

    p = _dot(xb, wp_ref[...])
    p_halo = _dot(xh_ref[...].astype(BF16), wp_ref[...])
    e_ref[0:POOL_HALO, :] = jnp.where(start == 0, 0.0, p_halo)
    e_ref[POOL_HALO:, :] = p

    pos = (start + 1 + lax.broadcasted_iota(jnp.int32, (tm, 1), 0)).astype(F32)
    ys = []
    for g, w in enumerate(POOL_WINDOWS):
        cols = slice(g * POOL_GROUP_DIM, (g + 1) * POOL_GROUP_DIM)
        acc = e_ref[POOL_HALO:POOL_HALO + tm, cols]
        for back in range(1, w):
            acc = acc + e_ref[POOL_HALO - back:POOL_HALO - back + tm, cols]
        pooled = acc / jnp.minimum(pos, float(w)) - e_ref[POOL_HALO:POOL_HALO + tm, cols]
        ys.append(_dot(pooled.astype(BF16), poolw_ref[g]))
    y = jnp.concatenate(ys, axis=-1) * pscale_ref[...]
    y_pool = _dot(y.astype(BF16), wbp_ref[...])

    gate_p = _dot(xb, wgp_ref[...])
    gate_a = _dot(xb, wga_ref[...])
    y_attn = _dot(o_ref[...], wba_ref[...])
    merged = jax.nn.sigmoid(gate_p) * y_pool + jax.nn.sigmoid(gate_a) * y_attn
    z = DN_ALPHA * x + _dot(merged.astype(BF16), wout_ref[...])
    x1 = _layer_norm(z, g_ref[...], b_ref[...])
    out_ref[...] = _xattn(x1, kv_ref, wq_ref, wo_ref, g2_ref, b2_ref)


def _mix_xattn(x2d, o, kvm, w_p, w_gp, w_ga, pool_w, pool_scale, w_bp, w_ba, w_out, g1, b1, w_cq, w_co, g2, b2, seq, tm):
    n_tok = x2d.shape[0]
    n_mem = kvm.shape[1]
    halo_blocks = tm // POOL_HALO

    def full(a):
        return pl.BlockSpec(a.shape, lambda i: (0,) * a.ndim)

    weights = (w_p, w_gp, w_ga, pool_w, pool_scale, w_bp, w_ba, w_out, g1, b1, w_cq, w_co, g2, b2)
    return pl.pallas_call(
        functools.partial(_mix_kernel, tm=tm, seq=seq),
        grid=(n_tok // tm,),
        in_specs=[
            pl.BlockSpec((tm, D_MODEL), lambda i: (i, 0)),
            pl.BlockSpec((POOL_HALO, D_MODEL), lambda i: (jnp.maximum(i * halo_blocks - 1, 0), 0)),
            pl.BlockSpec((tm, DA_WIDTH), lambda i: (i, 0)),
            pl.BlockSpec((1, n_mem, 2 * D_MODEL), lambda i: ((i * tm) // seq, 0, 0)),
        ] + [full(w) for w in weights],
        out_specs=pl.BlockSpec((tm, D_MODEL), lambda i: (i, 0)),
        out_shape=jax.ShapeDtypeStruct((n_tok, D_MODEL), F32),
        scratch_shapes=[pltpu.VMEM((tm + POOL_HALO, POOL_WIDTH), F32)],
        compiler_params=_params("parallel"),
        name="mix_xattn",
    )(x2d, x2d, o, kvm, *weights)


def _xattn(x, kv_ref, wq_ref, wo_ref, g_ref, b_ref):
    q = (_dot(x.astype(BF16), wq_ref[...]) * (XA_HEAD_DIM ** -0.5)).astype(BF16)
    kv = kv_ref[0]
    outs = []
    for h in range(XA_HEADS):
        cols = slice(h * XA_HEAD_DIM, (h + 1) * XA_HEAD_DIM)
        s = _dot_nt(q[:, cols], kv[:, cols])
        p = jnp.exp(s - jnp.max(s, axis=-1, keepdims=True))
        l = jnp.sum(p, axis=-1, keepdims=True)
        v = kv[:, D_MODEL + h * XA_HEAD_DIM:D_MODEL + (h + 1) * XA_HEAD_DIM]
        outs.append(_dot(p.astype(BF16), v) / l)
    o = jnp.concatenate(outs, axis=-1)
    z = DN_ALPHA * x + _dot(o.astype(BF16), wo_ref[...])
    return _layer_norm(z, g_ref[...], b_ref[...])


SUBLANES = 8


def _sort_network(n):
    def merge(lo, hi, r):
        step = 2 * r
        if step < hi - lo:
            yield from merge(lo, hi, step)
            yield from merge(lo + r, hi, step)
            yield from ((i, i + r) for i in range(lo + r, hi - r, step))
        else:
            yield (lo, lo + r)

    def sort(lo, hi):
        if hi - lo >= 1:
            mid = lo + (hi - lo) // 2
            yield from sort(lo, mid)
            yield from sort(mid + 1, hi)
            yield from merge(lo, hi, 1)

    return tuple(sort(0, n - 1))


def _exchange(v, i, j):
    if v[j] is None:
        return
    if v[i] is None:
        v[i], v[j] = v[j], None
        return
    v[i], v[j] = jnp.maximum(v[i], v[j]), jnp.minimum(v[i], v[j])


def _top_sorted(blocks, k):
    v = list(blocks) + [None] * (k - len(blocks))
    for i, j in _sort_network(k):
        _exchange(v, i, j)
    shift = SUBLANES // 2
    while shift >= 1:
        other = [None if a is None else pltpu.roll(a, shift, 0) for a in v]
        v = [b if a is None else (a if b is None else jnp.maximum(a, b)) for a, b in zip(v, reversed(other))]
        stride = k // 2
        while stride >= 1:
            for i in range(k):
                if not i & stride:
                    _exchange(v, i, i + stride)
            stride //= 2
        shift //= 2
    return v


def _route_kernel(x_ref, wq_ref, keys_ref, rank2_ref, e2_ref, count_ref, d_ref, *, heads):
    q_t = _dot_nt(wq_ref[...], x_ref[...].astype(BF16)).astype(BF16)
    for g in range(heads):
        q_g = q_t[g * PEER_QUERY_DIM:(g + 1) * PEER_QUERY_DIM]
        s1 = _dot(keys_ref[g, 0], q_g[:PEER_HALF])
        s2 = _dot(keys_ref[g, 1], q_g[PEER_HALF:])
        _route_head(s1, s2, g, rank2_ref, e2_ref, count_ref, d_ref)


def _route_head(s1, s2, g, rank2_ref, e2_ref, count_ref, d_ref):
    k = PEER_TOPK
    n_blocks = PEER_N_KEYS // SUBLANES
    s1_blocks = [s1[r * SUBLANES:(r + 1) * SUBLANES] for r in range(n_blocks)]
    s2_blocks = [s2[r * SUBLANES:(r + 1) * SUBLANES] for r in range(n_blocks)]
    v1 = _top_sorted(s1_blocks, k)
    v2 = _top_sorted(s2_blocks, k)

    row = lax.broadcasted_iota(jnp.int32, v1[0].shape, 0)

    def pack(vals):
        out = vals[0]
        for r in range(1, SUBLANES):
            out = jnp.where(row == r, vals[r], out)
        return out

    v1_lo, v1_hi, v2_hi = pack(v1[:SUBLANES]), pack(v1[SUBLANES:]), pack(v2[SUBLANES:])
    cand = ([v1_lo + v2[0], v1_hi + v2[0]] + [v1_lo + v2[b] for b in range(1, SUBLANES)] + [v2_hi + v1[0]])
    top = _top_sorted(cand, k)
    tau = top[k - 1]
    z = jnp.ones(tau.shape, F32)
    for c in top[1:]:
        z = z + jnp.exp(c - top[0])
    inv_z = 1.0 / z

    partners = []
    for a in range(k):
        n = jnp.zeros(tau.shape, F32)
        for b in range(k // (a + 1)):
            n = jnp.where(v1[a] + v2[b] >= tau, float(b + 1), n)
        partners.append(n)

    rank2, e2, count, d = [], [], [], []
    for blk1, blk2 in zip(s1_blocks, s2_blocks):
        r2 = jnp.full(blk2.shape, float(k), F32)
        cnt = jnp.zeros(blk1.shape, F32)
        for b in reversed(range(k)):
            r2 = jnp.where(blk2 >= v2[b], float(b), r2)
            cnt = jnp.where(blk1 >= v1[b], partners[b], cnt)
        rank2.append(r2)
        count.append(cnt)
        e2.append(jnp.exp(blk2 - v2[0]))
        d.append(jnp.exp(blk1 - v1[0]) * inv_z)
    rank2_ref[g] = jnp.concatenate(rank2, axis=0).astype(BF16)
    e2_ref[g] = jnp.concatenate(e2, axis=0).astype(BF16)
    count_ref[g] = jnp.concatenate(count, axis=0)
    d_ref[g] = jnp.concatenate(d, axis=0)


def _peer_route(x2, wq_t, keys, tm, heads):
    n_tok = x2.shape[0]
    out_spec = pl.BlockSpec((heads, PEER_N_KEYS, tm), lambda i, h: (h, 0, i))
    out_shape = [jax.ShapeDtypeStruct((PEER_HEADS, PEER_N_KEYS, n_tok), dt) for dt in (BF16, BF16, F32, F32)]
    return pl.pallas_call(
        functools.partial(_route_kernel, heads=heads),
        grid=(n_tok // tm, PEER_HEADS // heads),
        in_specs=[
            pl.BlockSpec((tm, D_MODEL), lambda i, h: (i, 0)),
            pl.BlockSpec((heads * PEER_QUERY_DIM, D_MODEL), lambda i, h: (h, 0)),
            pl.BlockSpec((heads, 2, PEER_N_KEYS, PEER_HALF), lambda i, h: (h, 0, 0, 0)),
        ],
        out_specs=[out_spec] * 4,
        out_shape=out_shape,
        compiler_params=_params("parallel", "parallel"),
        name="peer_route",
    )(x2, wq_t, keys)


def _peer_kernel(x_ref, u_ref, vt_ref, rank2_ref, e2_ref, count_ref, d_ref, g_ref, b_ref, out_ref,
                 xb_ref, acc_ref, a_ref, *, chunks, rows_per_chunk):
    e = pl.program_id(1)

    @pl.when(e == 0)
    def _():
        xb_ref[...] = (x_ref[...] * SQRT_HALF).astype(BF16)
        acc_ref[...] = jnp.zeros(acc_ref.shape, F32)

    xb = xb_ref[...]
    chunk = rows_per_chunk * PEER_N_KEYS
    tb = xb.shape[0]

    def sublane_rows(ref, h, row):
        tile = jnp.broadcast_to(ref[h, row:row + 1, :], (BF16_SUBLANES, tb)).astype(BF16)
        return jnp.concatenate([tile] * (PEER_N_KEYS // BF16_SUBLANES), axis=0)

    def activations(c):
        return _dot_nt(u_ref[c * chunk:(c + 1) * chunk, :], xb)

    a_ref[0] = activations(0)
    for c in range(chunks):
        if c + 1 < chunks:
            a_ref[(c + 1) % 2] = activations(c + 1)
        parts = []
        for r in range(rows_per_chunk):
            row = c * rows_per_chunk + r
            z = a_ref[c % 2, r * PEER_N_KEYS:(r + 1) * PEER_N_KEYS, :].astype(BF16)
            act = z + z * lax.erf(z)
            gate = None
            for h in range(PEER_HEADS):
                cnt = sublane_rows(count_ref, h, row)
                d = sublane_rows(d_ref, h, row)
                term = jnp.where(rank2_ref[h] < cnt, e2_ref[h], jnp.zeros((), BF16)) * d
                gate = term if gate is None else gate + term
            parts.append(gate * act)
        w = jnp.concatenate(parts, axis=0)
        acc_ref[...] += _dot(vt_ref[:, c * chunk:(c + 1) * chunk], w)

    @pl.when(e == pl.num_programs(1) - 1)
    def _():
        z = DN_ALPHA * x_ref[...] + SQRT_HALF * acc_ref[...].T
        out_ref[...] = _layer_norm(z, g_ref[...], b_ref[...])


def _peer_ln3(x2, u, vt, rank2, e2, count, d, g, b, tb, te, chunk):
    n_tok = x2.shape[0]
    rows_per_tile = te // PEER_N_KEYS
    key_spec = pl.BlockSpec((PEER_HEADS, PEER_N_KEYS, tb), lambda t, e: (0, 0, t))
    row_spec = pl.BlockSpec((PEER_HEADS, rows_per_tile, tb), lambda t, e: (0, e, t))
    vec = pl.BlockSpec((1, D_MODEL), lambda t, e: (0, 0))
    return pl.pallas_call(
        functools.partial(_peer_kernel, chunks=te // chunk, rows_per_chunk=chunk // PEER_N_KEYS),
        grid=(n_tok // tb, PEER_N_EXPERTS // te),
        in_specs=[
            pl.BlockSpec((tb, D_MODEL), lambda t, e: (t, 0)),
            pl.BlockSpec((te, D_MODEL), lambda t, e: (e, 0)),
            pl.BlockSpec((D_MODEL, te), lambda t, e: (0, e)),
            key_spec, key_spec, row_spec, row_spec,
            vec, vec,
        ],
        out_specs=pl.BlockSpec((tb, D_MODEL), lambda t, e: (t, 0)),
        out_shape=jax.ShapeDtypeStruct((n_tok, D_MODEL), F32),
        scratch_shapes=[
            pltpu.VMEM((tb, D_MODEL), BF16),
            pltpu.VMEM((D_MODEL, tb), F32),
            pltpu.VMEM((2, chunk, tb), F32),
        ],
        compiler_params=_params("parallel", "arbitrary"),
        name="peer_dense",
    )(x2, u, vt, rank2, e2, count, d, g, b)


def kernel(x, mem, w_in, pool_w, pool_scale, w_br_pool, lambda_q1, lambda_k1, lambda_q2, lambda_k2, subln_w, w_br_attn, w_out, ln1_g, ln1_b, w_cq, w_ckv, w_co, ln2_g, ln2_b, w_pq, sub_keys, expert_u, expert_v, ln3_g, ln3_b):
    batch, seq, _ = x.shape
    n_tok = batch * seq
    n_mem = mem.shape[1]
    tiles = TILES
    assert seq % tiles.attn == 0 and seq % tiles.mix == 0 and n_tok % tiles.peer_tokens == 0
    slopes = jnp.exp2(-8.0 * jnp.arange(1, DA_HEADS + 1, dtype=F32) / DA_HEADS)

    def row(v):
        return v.reshape(1, -1).astype(F32)

    for l in range(DEPTH):
        lam_init = 0.8 - 0.6 * math.exp(-0.3 * l)
        x2d = x.reshape(n_tok, D_MODEL)
        w_in_b = w_in[l].astype(BF16)
        q0 = POOL_WIDTH
        k0 = q0 + DA_WIDTH
        v0 = k0 + DA_WIDTH
        gp0 = v0 + DA_WIDTH
        ga0 = gp0 + D_MODEL

        qk_scale = jnp.concatenate([jnp.full((1, DA_WIDTH), LOG2E * DA_QK_DIM ** -0.5, F32),
                                    jnp.ones((1, DA_WIDTH), F32)], axis=1)
        qk, vtb = _proj_qkv(x2d, w_in_b[:, q0:v0], qk_scale, w_in_b[:, v0:gp0].T, tiles.attn)
        o = _diff_attention(qk, vtb, slopes, row(lambda_q1[l]), row(lambda_k1[l]), row(lambda_q2[l]),
                            row(lambda_k2[l]), row(subln_w[l]), batch, seq, lam_init, tiles.attn, tiles.attn_heads)

        kvm = _proj(mem.reshape(batch * n_mem, D_MODEL), w_ckv[l].astype(BF16), tiles.mem_cols,
                    "proj_mem_kv").reshape(batch, n_mem, 2 * D_MODEL)
        x2 = _mix_xattn(x2d, o, kvm, w_in_b[:, :POOL_WIDTH], w_in_b[:, gp0:ga0], w_in_b[:, ga0:],
                        pool_w[l].astype(BF16), row(pool_scale[l]), w_br_pool[l].astype(BF16),
                        w_br_attn[l].astype(BF16), w_out[l].astype(BF16), row(ln1_g[l]), row(ln1_b[l]),
                        w_cq[l].astype(BF16), w_co[l].astype(BF16), row(ln2_g[l]), row(ln2_b[l]), seq, tiles.mix)

        rank2, e2, count, d = _peer_route(x2, w_pq[l].T.astype(BF16), sub_keys[l].astype(BF16),
                                          tiles.peer_tokens, tiles.route_heads)
        x3 = _peer_ln3(x2, expert_u[l].astype(BF16), expert_v[l].T.astype(BF16), rank2, e2, count, d,
                       row(ln3_g[l]), row(ln3_b[l]), tiles.peer_tokens, tiles.peer_experts, tiles.peer_chunk)
        x = x3.reshape(batch, seq, D_MODEL)
    return x
```

```python
import functools
import math
from typing import NamedTuple

import jax
import jax.numpy as jnp
import numpy as np
from jax import lax
from jax.experimental import pallas as pl
from jax.experimental.pallas import tpu as pltpu

F32 = jnp.float32
BF16 = jnp.bfloat16

D_MODEL = 1024
POOL_WINDOWS = (2, 4, 8, 16)
POOL_GROUPS = len(POOL_WINDOWS)
POOL_WIDTH = D_MODEL // 2
POOL_GROUP_DIM = POOL_WIDTH // POOL_GROUPS
POOL_HALO = max(POOL_WINDOWS)
DA_HEADS = 8
DA_QK_DIM = 64
DA_V_DIM = 2 * DA_QK_DIM
DA_WIDTH = DA_HEADS * DA_V_DIM
XA_HEADS = 4
XA_HEAD_DIM = D_MODEL // XA_HEADS
PEER_HEADS = 8
PEER_N_KEYS = 128
PEER_N_EXPERTS = PEER_N_KEYS * PEER_N_KEYS
PEER_QUERY_DIM = 128
PEER_HALF = PEER_QUERY_DIM // 2
PEER_TOPK = 16
DEPTH = 1
DN_ALPHA = (2 * DEPTH) ** 0.25
LN_EPS = 1e-5
NEG_INF = -1e30
SQRT_HALF = math.sqrt(0.5)
LOG2E = float(np.float32(math.log2(math.e)))

VMEM_LIMIT_BYTES = 52 * 1024 * 1024
BF16_SUBLANES = 16


class Tiles(NamedTuple):
    attn: int = 512
    attn_heads: int = 2
    mix: int = 512
    mem_cols: int = 512
    peer_tokens: int = 512
    route_heads: int = 8
    peer_experts: int = 2048
    peer_chunk: int = 1024


TILES = Tiles()

_NT = (((1,), (1,)), ((), ()))


def _params(*sem):
    return pltpu.CompilerParams(dimension_semantics=sem, vmem_limit_bytes=VMEM_LIMIT_BYTES)


def _dot(a, b):
    return jnp.dot(a, b, preferred_element_type=F32)


def _dot_nt(a, b):
    return lax.dot_general(a, b, _NT, preferred_element_type=F32)


def _layer_norm(z, g, b):
    mu = jnp.mean(z, axis=-1, keepdims=True)
    zc = z - mu
    var = jnp.mean(zc * zc, axis=-1, keepdims=True)
    return zc * lax.rsqrt(var + LN_EPS) * g + b


def _proj_kernel(a_ref, w_ref, o_ref):
    o_ref[...] = _dot(a_ref[...].astype(BF16), w_ref[...]).astype(o_ref.dtype)


def _proj(a, w, tn, name):
    m, k = a.shape
    n = w.shape[1]
    return pl.pallas_call(
        _proj_kernel,
        grid=(n // tn,),
        in_specs=[pl.BlockSpec((m, k), lambda j: (0, 0)), pl.BlockSpec((k, tn), lambda j: (0, j))],
        out_specs=pl.BlockSpec((m, tn), lambda j: (0, j)),
        out_shape=jax.ShapeDtypeStruct((m, n), BF16),
        compiler_params=_params("parallel"),
        name=name,
    )(a, w)


def _proj_qkv_kernel(x_ref, wqk_ref, scale_ref, wvt_ref, qk_ref, vt_ref):
    xb = x_ref[...].astype(BF16)
    qk_ref[...] = (_dot(xb, wqk_ref[...]) * scale_ref[...]).astype(qk_ref.dtype)
    vt_ref[0] = _dot_nt(wvt_ref[...], xb).astype(vt_ref.dtype)


def _proj_qkv(x2d, w_qk, qk_scale, w_v_t, tk):
    m, k = x2d.shape
    n = w_qk.shape[1]
    nv = w_v_t.shape[0]
    return pl.pallas_call(
        _proj_qkv_kernel,
        grid=(m // tk,),
        in_specs=[
            pl.BlockSpec((tk, k), lambda j: (j, 0)),
            pl.BlockSpec((k, n), lambda j: (0, 0)),
            pl.BlockSpec((1, n), lambda j: (0, 0)),
            pl.BlockSpec((nv, k), lambda j: (0, 0)),
        ],
        out_specs=[pl.BlockSpec((tk, n), lambda j: (j, 0)), pl.BlockSpec((1, nv, tk), lambda j: (j, 0, 0))],
        out_shape=[jax.ShapeDtypeStruct((m, n), BF16), jax.ShapeDtypeStruct((m // tk, nv, tk), BF16)],
        compiler_params=_params("parallel"),
        name="proj_qkv",
    )(x2d, w_qk, qk_scale, w_v_t)


ATTN_ONES_ROWS = 16
ATTN_SKIP_LOG2 = 160.0
ALIBI_PIECES = 5


def _alibi_tables(t):
    slopes = np.exp2(-8.0 * np.arange(1, DA_HEADS + 1, dtype=np.float64) / DA_HEADS)
    ramp = float(LOG2E) * slopes[:, None] * np.arange(t, dtype=np.float64)[None, :]

    def pieces(v):
        out = []
        for _ in range(ALIBI_PIECES):
            piece = v.astype(np.float32).astype(BF16).astype(np.float64)
            out.append(piece)
            v = v - piece
        assert not v.any()
        return out

    ones = np.ones_like(ramp)
    pad = [np.zeros_like(ramp)] * (DA_V_DIM - 2 * ALIBI_PIECES)
    key_side = np.stack(pieces(ramp) + [ones] * ALIBI_PIECES + pad, axis=-1)
    query_side = np.stack([ones] * ALIBI_PIECES + pieces(-ramp) + pad, axis=-1)
    return jnp.asarray(key_side, BF16), jnp.asarray(query_side, BF16)


def _attn_kernel(slopes_ref, inv_tile_ref, lq1_ref, lk1_ref, lq2_ref, lk2_ref, sw_ref, q_ref, qb_ref, k_ref, kb_ref,
                 vt_ref, causal_ref, o_ref, sa_ref, sb_ref, m_ref, acc_ref, kmax_ref, *, t, heads, lam_init):
    hg = pl.program_id(1)
    i = pl.program_id(2)
    slope_log2 = [slopes_ref[hg * heads + g] * LOG2E for g in range(heads)]

    lane = lax.broadcasted_iota(jnp.int32, (t, DA_V_DIM), 1)
    q_maps = []
    for g in range(heads):
        q = q_ref[:, g * DA_V_DIM:(g + 1) * DA_V_DIM]
        zero = jnp.zeros_like(q)
        q_maps += [jnp.concatenate([jnp.where(lane < DA_QK_DIM, q, zero), qb_ref[g]], axis=1),
                   jnp.concatenate([jnp.where(lane >= DA_QK_DIM, q, zero), qb_ref[g]], axis=1)]
    ones_rows = jnp.ones((ATTN_ONES_ROWS, t), BF16)

    def scores(j, s_ref):
        rows = pl.ds(pl.multiple_of(j * t, t), t)
        for g in range(heads):
            k_aug = jnp.concatenate([k_ref[rows, g * DA_V_DIM:(g + 1) * DA_V_DIM], kb_ref[g]], axis=1)
            for mp in range(2):
                s_ref[2 * g + mp] = _dot_nt(k_aug, q_maps[2 * g + mp])

    def consume(j, s_ref, diagonal=False):
        for g in range(heads):
            vt_aug = jnp.concatenate([vt_ref[j, g * DA_V_DIM:(g + 1) * DA_V_DIM, :], ones_rows], axis=0)
            shift = slope_log2[g] * ((j - i) * t).astype(F32)
            for mp in range(2):
                c = 2 * g + mp
                s = s_ref[c]
                if diagonal:
                    s = s + causal_ref[...]
                    m_new = jnp.max(s, axis=0, keepdims=True)
                    acc_ref[c] = _dot(vt_aug, jnp.exp2(s - m_new).astype(BF16))
                    m_ref[c] = m_new
                    continue
                m_old = m_ref[c]
                m_new = jnp.maximum(m_old, jnp.max(s, axis=0, keepdims=True) + shift)
                p = jnp.exp2(s - (m_new - shift))
                acc_ref[c] = jnp.exp2(m_old - m_new) * acc_ref[c] + _dot(vt_aug, p.astype(BF16))
                m_ref[c] = m_new

    def largest_row_norm(rows):
        sq = jnp.square(rows.astype(F32))
        return jnp.max(jnp.sqrt(jnp.max(jnp.sum(sq, axis=1, keepdims=True), axis=0, keepdims=True)))

    @pl.when(i == 0)
    def _():
        for g in range(heads):
            kmax_ref[g] = largest_row_norm(k_ref[:, g * DA_V_DIM:(g + 1) * DA_V_DIM])

    scores(i, sa_ref)
    scores(jnp.maximum(i - 1, 0), sb_ref)
    consume(i, sa_ref, diagonal=True)

    keep = jnp.int32(0)
    for g in range(heads):
        m_min = jnp.min(m_ref[2 * g:2 * g + 2])
        q_max = largest_row_norm(q_ref[:, g * DA_V_DIM:(g + 1) * DA_V_DIM])
        reach = (ATTN_SKIP_LOG2 + q_max * kmax_ref[g] - m_min) * inv_tile_ref[hg * heads + g] + (t - 1) / t
        keep = jnp.maximum(keep, jnp.minimum(reach, float(2 ** 20)).astype(jnp.int32))
    keep = jnp.clip(keep, 0, i)

    def tile_pair(jj, carry):
        dist = 1 + 2 * jj
        scores(jnp.maximum(i - dist - 1, 0), sa_ref)
        consume(i - dist, sb_ref)
        scores(jnp.maximum(i - dist - 2, 0), sb_ref)
        consume(i - dist - 1, sa_ref)
        return carry

    lax.fori_loop(0, keep // 2, tile_pair, 0)

    @pl.when(keep % 2 == 1)
    def _():
        consume(i - keep, sb_ref)

    lam = (jnp.exp(jnp.sum(lq1_ref[...] * lk1_ref[...], axis=-1, keepdims=True))
           - jnp.exp(jnp.sum(lq2_ref[...] * lk2_ref[...], axis=-1, keepdims=True)) + lam_init)
    for g in range(heads):
        a1 = acc_ref[2 * g]
        a2 = acc_ref[2 * g + 1]
        o_t = (a1[:DA_V_DIM] / a1[DA_V_DIM:DA_V_DIM + 1]
               - lam * (a2[:DA_V_DIM] / a2[DA_V_DIM:DA_V_DIM + 1]))
        o = o_t.T
        o = o * lax.rsqrt(jnp.mean(o * o, axis=-1, keepdims=True) + LN_EPS)
        o_ref[:, g * DA_V_DIM:(g + 1) * DA_V_DIM] = (o * sw_ref[...] * (1.0 - lam_init)).astype(o_ref.dtype)


def _diff_attention(qk, vtb, slopes, lq1, lk1, lq2, lk2, subln_w, batch, seq, lam_init, t, heads):
    n_tok = qk.shape[0]
    nq = seq // t
    width = heads * DA_V_DIM
    key_bias, query_bias = _alibi_tables(t)
    pos = np.arange(t)
    causal = jnp.asarray(np.where(pos[:, None] <= pos[None, :], 0.0, NEG_INF), F32)
    vec = pl.BlockSpec((1, DA_QK_DIM), lambda b, h, i: (0, 0))
    bias_spec = pl.BlockSpec((heads, t, DA_V_DIM), lambda b, h, i: (h, 0, 0))
    maps = 2 * heads
    return pl.pallas_call(
        functools.partial(_attn_kernel, t=t, heads=heads, lam_init=lam_init),
        grid=(batch, DA_HEADS // heads, nq),
        in_specs=[
            pl.BlockSpec(memory_space=pltpu.SMEM),
            pl.BlockSpec(memory_space=pltpu.SMEM),
            vec, vec, vec, vec,
            pl.BlockSpec((1, DA_V_DIM), lambda b, h, i: (0, 0)),
            pl.BlockSpec((t, width), lambda b, h, i: (b * nq + i, h)),
            bias_spec,
            pl.BlockSpec((seq, width), lambda b, h, i: (b, DA_HEADS // heads + h)),
            bias_spec,
            pl.BlockSpec((nq, width, t), lambda b, h, i: (b, h, 0)),
            pl.BlockSpec((t, t), lambda b, h, i: (0, 0)),
        ],
        out_specs=pl.BlockSpec((t, width), lambda b, h, i: (b * nq + i, h)),
        out_shape=jax.ShapeDtypeStruct((n_tok, DA_WIDTH), BF16),
        scratch_shapes=[
            pltpu.VMEM((maps, t, t), F32),
            pltpu.VMEM((maps, t, t), F32),
            pltpu.VMEM((maps, 1, t), F32),
            pltpu.VMEM((maps, DA_V_DIM + ATTN_ONES_ROWS, t), F32),
            pltpu.SMEM((heads,), F32),
        ],
        compiler_params=_params("parallel", "parallel", "arbitrary"),
        name="diff_attention",
    )(slopes, 1.0 / (slopes * (LOG2E * t)), lq1, lk1, lq2, lk2, subln_w, qk, query_bias, qk, key_bias, vtb, causal)


def _mix_kernel(x_ref, xh_ref, o_ref, kv_ref, wp_ref, wgp_ref, wga_ref, poolw_ref, pscale_ref, wbp_ref, wba_ref,
                wout_ref, g_ref, b_ref, wq_ref, wo_ref, g2_ref, b2_ref, out_ref, e_ref, *, tm, seq):
    i = pl.program_id(0)
    start = (i * tm) % seq
    x = x_ref[...]
    xb = x.astype(BF16)

    p = _dot(xb, wp_ref[...])
    p_halo = _dot(xh_ref[...].astype(BF16), wp_ref[...])
    e_ref[0:POOL_HALO, :] = jnp.where(start == 0, 0.0, p_halo)
    e_ref[POOL_HALO:, :] = p

    pos = (start + 1 + lax.broadcasted_iota(jnp.int32, (tm, 1), 0)).astype(F32)
    ys = []
    for g, w in enumerate(POOL_WINDOWS):
        cols = slice(g * POOL_GROUP_DIM, (g + 1) * POOL_GROUP_DIM)
        acc = e_ref[POOL_HALO:POOL_HALO + tm, cols]
        for back in range(1, w):
            acc = acc + e_ref[POOL_HALO - back:POOL_HALO - back + tm, cols]
        pooled = acc / jnp.minimum(pos, float(w)) - e_ref[POOL_HALO:POOL_HALO + tm, cols]
        ys.append(_dot(pooled.astype(BF16), poolw_ref[g]))
    y = jnp.concatenate(ys, axis=-1) * pscale_ref[...]
    y_pool = _dot(y.astype(BF16), wbp_ref[...])

    gate_p = _dot(xb, wgp_ref[...])
    gate_a = _dot(xb, wga_ref[...])
    y_attn = _dot(o_ref[...], wba_ref[...])
    merged = jax.nn.sigmoid(gate_p) * y_pool + jax.nn.sigmoid(gate_a) * y_attn
    z = DN_ALPHA * x + _dot(merged.astype(BF16), wout_ref[...])
    x1 = _layer_norm(z, g_ref[...], b_ref[...])
    out_ref[...] = _xattn(x1, kv_ref, wq_ref, wo_ref, g2_ref, b2_ref)


def _mix_xattn(x2d, o, kvm, w_p, w_gp, w_ga, pool_w, pool_scale, w_bp, w_ba, w_out, g1, b1, w_cq, w_co, g2, b2, seq, tm):
    n_tok = x2d.shape[0]
    n_mem = kvm.shape[1]
    halo_blocks = tm // POOL_HALO

    def full(a):
        return pl.BlockSpec(a.shape, lambda i: (0,) * a.ndim)

    weights = (w_p, w_gp, w_ga, pool_w, pool_scale, w_bp, w_ba, w_out, g1, b1, w_cq, w_co, g2, b2)
    return pl.pallas_call(
        functools.partial(_mix_kernel, tm=tm, seq=seq),
        grid=(n_tok // tm,),
        in_specs=[
            pl.BlockSpec((tm, D_MODEL), lambda i: (i, 0)),
            pl.BlockSpec((POOL_HALO, D_MODEL), lambda i: (jnp.maximum(i * halo_blocks - 1, 0), 0)),
            pl.BlockSpec((tm, DA_WIDTH), lambda i: (i, 0)),
            pl.BlockSpec((1, n_mem, 2 * D_MODEL), lambda i: ((i * tm) // seq, 0, 0)),
        ] + [full(w) for w in weights],
        out_specs=pl.BlockSpec((tm, D_MODEL), lambda i: (i, 0)),
        out_shape=jax.ShapeDtypeStruct((n_tok, D_MODEL), F32),
        scratch_shapes=[pltpu.VMEM((tm + POOL_HALO, POOL_WIDTH), F32)],
        compiler_params=_params("parallel"),
        name="mix_xattn",
    )(x2d, x2d, o, kvm, *weights)


def _xattn(x, kv_ref, wq_ref, wo_ref, g_ref, b_ref):
    q = (_dot(x.astype(BF16), wq_ref[...]) * (XA_HEAD_DIM ** -0.5)).astype(BF16)
    kv = kv_ref[0]
    outs = []
    for h in range(XA_HEADS):
        cols = slice(h * XA_HEAD_DIM, (h + 1) * XA_HEAD_DIM)
        s = _dot_nt(q[:, cols], kv[:, cols])
        p = jnp.exp(s - jnp.max(s, axis=-1, keepdims=True))
        l = jnp.sum(p, axis=-1, keepdims=True)
        v = kv[:, D_MODEL + h * XA_HEAD_DIM:D_MODEL + (h + 1) * XA_HEAD_DIM]
        outs.append(_dot(p.astype(BF16), v) / l)
    o = jnp.concatenate(outs, axis=-1)
    z = DN_ALPHA * x + _dot(o.astype(BF16), wo_ref[...])
    return _layer_norm(z, g_ref[...], b_ref[...])


SUBLANES = 8


def _sort_network(n):
    def merge(lo, hi, r):
        step = 2 * r
        if step < hi - lo:
            yield from merge(lo, hi, step)
            yield from merge(lo + r, hi, step)
            yield from ((i, i + r) for i in range(lo + r, hi - r, step))
        else:
            yield (lo, lo + r)

    def sort(lo, hi):
        if hi - lo >= 1:
            mid = lo + (hi - lo) // 2
            yield from sort(lo, mid)
            yield from sort(mid + 1, hi)
            yield from merge(lo, hi, 1)

    return tuple(sort(0, n - 1))


def _exchange(v, i, j):
    if v[j] is None:
        return
    if v[i] is None:
        v[i], v[j] = v[j], None
        return
    v[i], v[j] = jnp.maximum(v[i], v[j]), jnp.minimum(v[i], v[j])


def _top_sorted(blocks, k):
    v = list(blocks) + [None] * (k - len(blocks))
    for i, j in _sort_network(k):
        _exchange(v, i, j)
    shift = SUBLANES // 2
    while shift >= 1:
        other = [None if a is None else pltpu.roll(a, shift, 0) for a in v]
        v = [b if a is None else (a if b is None else jnp.maximum(a, b)) for a, b in zip(v, reversed(other))]
        stride = k // 2
        while stride >= 1:
            for i in range(k):
                if not i & stride:
                    _exchange(v, i, i + stride)
            stride //= 2
        shift //= 2
    return v


def _route_kernel(x_ref, wq_ref, keys_ref, rank2_ref, e2_ref, count_ref, d_ref, *, heads):
    q_t = _dot_nt(wq_ref[...], x_ref[...].astype(BF16)).astype(BF16)
    for g in range(heads):
        q_g = q_t[g * PEER_QUERY_DIM:(g + 1) * PEER_QUERY_DIM]
        s1 = _dot(keys_ref[g, 0], q_g[:PEER_HALF])
        s2 = _dot(keys_ref[g, 1], q_g[PEER_HALF:])
        _route_head(s1, s2, g, rank2_ref, e2_ref, count_ref, d_ref)


def _route_head(s1, s2, g, rank2_ref, e2_ref, count_ref, d_ref):
    k = PEER_TOPK
    n_blocks = PEER_N_KEYS // SUBLANES
    s1_blocks = [s1[r * SUBLANES:(r + 1) * SUBLANES] for r in range(n_blocks)]
    s2_blocks = [s2[r * SUBLANES:(r + 1) * SUBLANES] for r in range(n_blocks)]
    v1 = _top_sorted(s1_blocks, k)
    v2 = _top_sorted(s2_blocks, k)

    row = lax.broadcasted_iota(jnp.int32, v1[0].shape, 0)

    def pack(vals):
        out = vals[0]
        for r in range(1, SUBLANES):
            out = jnp.where(row == r, vals[r], out)
        return out

    v1_lo, v1_hi, v2_hi = pack(v1[:SUBLANES]), pack(v1[SUBLANES:]), pack(v2[SUBLANES:])
    cand = ([v1_lo + v2[0], v1_hi + v2[0]] + [v1_lo + v2[b] for b in range(1, SUBLANES)] + [v2_hi + v1[0]])
    top = _top_sorted(cand, k)
    tau = top[k - 1]
    z = jnp.ones(tau.shape, F32)
    for c in top[1:]:
        z = z + jnp.exp(c - top[0])
    inv_z = 1.0 / z

    partners = []
    for a in range(k):
        n = jnp.zeros(tau.shape, F32)
        for b in range(k // (a + 1)):
            n = jnp.where(v1[a] + v2[b] >= tau, float(b + 1), n)
        partners.append(n)

    rank2, e2, count, d = [], [], [], []
    for blk1, blk2 in zip(s1_blocks, s2_blocks):
        r2 = jnp.full(blk2.shape, float(k), F32)
        cnt = jnp.zeros(blk1.shape, F32)
        for b in reversed(range(k)):
            r2 = jnp.where(blk2 >= v2[b], float(b), r2)
            cnt = jnp.where(blk1 >= v1[b], partners[b], cnt)
        rank2.append(r2)
        count.append(cnt)
        e2.append(jnp.exp(blk2 - v2[0]))
        d.append(jnp.exp(blk1 - v1[0]) * inv_z)
    rank2_ref[g] = jnp.concatenate(rank2, axis=0).astype(BF16)
    e2_ref[g] = jnp.concatenate(e2, axis=0).astype(BF16)
    count_ref[g] = jnp.concatenate(count, axis=0)
    d_ref[g] = jnp.concatenate(d, axis=0)


def _peer_route(x2, wq_t, keys, tm, heads):
    n_tok = x2.shape[0]
    out_spec = pl.BlockSpec((heads, PEER_N_KEYS, tm), lambda i, h: (h, 0, i))
    out_shape = [jax.ShapeDtypeStruct((PEER_HEADS, PEER_N_KEYS, n_tok), dt) for dt in (BF16, BF16, F32, F32)]
    return pl.pallas_call(
        functools.partial(_route_kernel, heads=heads),
        grid=(n_tok // tm, PEER_HEADS // heads),
        in_specs=[
            pl.BlockSpec((tm, D_MODEL), lambda i, h: (i, 0)),
            pl.BlockSpec((heads * PEER_QUERY_DIM, D_MODEL), lambda i, h: (h, 0)),
            pl.BlockSpec((heads, 2, PEER_N_KEYS, PEER_HALF), lambda i, h: (h, 0, 0, 0)),
        ],
        out_specs=[out_spec] * 4,
        out_shape=out_shape,
        compiler_params=_params("parallel", "parallel"),
        name="peer_route",
    )(x2, wq_t, keys)


def _peer_kernel(x_ref, u_ref, vt_ref, rank2_ref, e2_ref, count_ref, d_ref, g_ref, b_ref, out_ref,
                 xb_ref, acc_ref, a_ref, *, chunks, rows_per_chunk):
    e = pl.program_id(1)

    @pl.when(e == 0)
    def _():
        xb_ref[...] = (x_ref[...] * SQRT_HALF).astype(BF16)
        acc_ref[...] = jnp.zeros(acc_ref.shape, F32)

    xb = xb_ref[...]
    chunk = rows_per_chunk * PEER_N_KEYS
    tb = xb.shape[0]

    def sublane_rows(ref, h, row):
        tile = jnp.broadcast_to(ref[h, row:row + 1, :], (BF16_SUBLANES, tb)).astype(BF16)
        return jnp.concatenate([tile] * (PEER_N_KEYS // BF16_SUBLANES), axis=0)

    def activations(c):
        return _dot_nt(u_ref[c * chunk:(c + 1) * chunk, :], xb)

    a_ref[0] = activations(0)
    for c in range(chunks):
        if c + 1 < chunks:
            a_ref[(c + 1) % 2] = activations(c + 1)
        parts = []
        for r in range(rows_per_chunk):
            row = c * rows_per_chunk + r
            z = a_ref[c % 2, r * PEER_N_KEYS:(r + 1) * PEER_N_KEYS, :].astype(BF16)
            act = z + z * lax.erf(z)
            gate = None
            for h in range(PEER_HEADS):
                cnt = sublane_rows(count_ref, h, row)
                d = sublane_rows(d_ref, h, row)
                term = jnp.where(rank2_ref[h] < cnt, e2_ref[h], jnp.zeros((), BF16)) * d
                gate = term if gate is None else gate + term
            parts.append(gate * act)
        w = jnp.concatenate(parts, axis=0)
        acc_ref[...] += _dot(vt_ref[:, c * chunk:(c + 1) * chunk], w)

    @pl.when(e == pl.num_programs(1) - 1)
    def _():
        z = DN_ALPHA * x_ref[...] + SQRT_HALF * acc_ref[...].T
        out_ref[...] = _layer_norm(z, g_ref[...], b_ref[...])


def _peer_ln3(x2, u, vt, rank2, e2, count, d, g, b, tb, te, chunk):
    n_tok = x2.shape[0]
    rows_per_tile = te // PEER_N_KEYS
    key_spec = pl.BlockSpec((PEER_HEADS, PEER_N_KEYS, tb), lambda t, e: (0, 0, t))
    row_spec = pl.BlockSpec((PEER_HEADS, rows_per_tile, tb), lambda t, e: (0, e, t))
    vec = pl.BlockSpec((1, D_MODEL), lambda t, e: (0, 0))
    return pl.pallas_call(
        functools.partial(_peer_kernel, chunks=te // chunk, rows_per_chunk=chunk // PEER_N_KEYS),
        grid=(n_tok // tb, PEER_N_EXPERTS // te),
        in_specs=[
            pl.BlockSpec((tb, D_MODEL), lambda t, e: (t, 0)),
            pl.BlockSpec((te, D_MODEL), lambda t, e: (e, 0)),
            pl.BlockSpec((D_MODEL, te), lambda t, e: (0, e)),
            key_spec, key_spec, row_spec, row_spec,
            vec, vec,
        ],
        out_specs=pl.BlockSpec((tb, D_MODEL), lambda t, e: (t, 0)),
        out_shape=jax.ShapeDtypeStruct((n_tok, D_MODEL), F32),
        scratch_shapes=[
            pltpu.VMEM((tb, D_MODEL), BF16),
            pltpu.VMEM((D_MODEL, tb), F32),
            pltpu.VMEM((2, chunk, tb), F32),
        ],
        compiler_params=_params("parallel", "arbitrary"),
        name="peer_dense",
    )(x2, u, vt, rank2, e2, count, d, g, b)


def kernel(x, mem, w_in, pool_w, pool_scale, w_br_pool, lambda_q1, lambda_k1, lambda_q2, lambda_k2, subln_w, w_br_attn, w_out, ln1_g, ln1_b, w_cq, w_ckv, w_co, ln2_g, ln2_b, w_pq, sub_keys, expert_u, expert_v, ln3_g, ln3_b):
    batch, seq, _ = x.shape
    n_tok = batch * seq
    n_mem = mem.shape[1]
    tiles = TILES
    assert seq % tiles.attn == 0 and seq % tiles.mix == 0 and n_tok % tiles.peer_tokens == 0
    slopes = jnp.exp2(-8.0 * jnp.arange(1, DA_HEADS + 1, dtype=F32) / DA_HEADS)

    def row(v):
        return v.reshape(1, -1).astype(F32)

    for l in range(DEPTH):
        lam_init = 0.8 - 0.6 * math.exp(-0.3 * l)
        x2d = x.reshape(n_tok, D_MODEL)
        w_in_b = w_in[l].astype(BF16)
        q0 = POOL_WIDTH
        k0 = q0 + DA_WIDTH
        v0 = k0 + DA_WIDTH
        gp0 = v0 + DA_WIDTH
        ga0 = gp0 + D_MODEL

        qk_scale = jnp.concatenate([jnp.full((1, DA_WIDTH), LOG2E * DA_QK_DIM ** -0.5, F32),
                                    jnp.ones((1, DA_WIDTH), F32)], axis=1)
        qk, vtb = _proj_qkv(x2d, w_in_b[:, q0:v0], qk_scale, w_in_b[:, v0:gp0].T, tiles.attn)
        o = _diff_attention(qk, vtb, slopes, row(lambda_q1[l]), row(lambda_k1[l]), row(lambda_q2[l]),
                            row(lambda_k2[l]), row(subln_w[l]), batch, seq, lam_init, tiles.attn, tiles.attn_heads)

        kvm = _proj(mem.reshape(batch * n_mem, D_MODEL), w_ckv[l].astype(BF16), tiles.mem_cols,
                    "proj_mem_kv").reshape(batch, n_mem, 2 * D_MODEL)
        x2 = _mix_xattn(x2d, o, kvm, w_in_b[:, :POOL_WIDTH], w_in_b[:, gp0:ga0], w_in_b[:, ga0:],
                        pool_w[l].astype(BF16), row(pool_scale[l]), w_br_pool[l].astype(BF16),
                        w_br_attn[l].astype(BF16), w_out[l].astype(BF16), row(ln1_g[l]), row(ln1_b[l]),
                        w_cq[l].astype(BF16), w_co[l].astype(BF16), row(ln2_g[l]), row(ln2_b[l]), seq, tiles.mix)

        rank2, e2, count, d = _peer_route(x2, w_pq[l].T.astype(BF16), sub_keys[l].astype(BF16),
                                          tiles.peer_tokens, tiles.route_heads)
        x3 = _peer_ln3(x2, expert_u[l].astype(BF16), expert_v[l].T.astype(BF16), rank2, e2, count, d,
                       row(ln3_g[l]), row(ln3_b[l]), tiles.peer_tokens, tiles.peer_experts, tiles.peer_chunk)
        x = x3.reshape(batch, seq, D_MODEL)
    return x
```

```python
import functools
import math
from typing import NamedTuple

import jax
import jax.numpy as jnp
import numpy as np
from jax import lax
from jax.experimental import pallas as pl
from jax.experimental.pallas import tpu as pltpu

F32 = jnp.float32
BF16 = jnp.bfloat16

D_MODEL = 1024
POOL_WINDOWS = (2, 4, 8, 16)
POOL_GROUPS = len(POOL_WINDOWS)
POOL_WIDTH = D_MODEL // 2
POOL_GROUP_DIM = POOL_WIDTH // POOL_GROUPS
POOL_HALO = max(POOL_WINDOWS)
DA_HEADS = 8
DA_QK_DIM = 64
DA_V_DIM = 2 * DA_QK_DIM
DA_WIDTH = DA_HEADS * DA_V_DIM
XA_HEADS = 4
XA_HEAD_DIM = D_MODEL // XA_HEADS
PEER_HEADS = 8
PEER_N_KEYS = 128
PEER_N_EXPERTS = PEER_N_KEYS * PEER_N_KEYS
PEER_QUERY_DIM = 128
PEER_HALF = PEER_QUERY_DIM // 2
PEER_TOPK = 16
DEPTH = 1
DN_ALPHA = (2 * DEPTH) ** 0.25
LN_EPS = 1e-5
NEG_INF = -1e30
SQRT_HALF = math.sqrt(0.5)
LOG2E = float(np.float32(math.log2(math.e)))

VMEM_LIMIT_BYTES = 52 * 1024 * 1024
BF16_SUBLANES = 16


class Tiles(NamedTuple):
    attn: int = 512
    attn_heads: int = 2
    mix: int = 1024
    mem_cols: int = 512
    peer_tokens: int = 512
    route_heads: int = 8
    peer_experts: int = 2048
    peer_chunk: int = 1024


TILES = Tiles()

_NT = (((1,), (1,)), ((), ()))


def _params(*sem):
    return pltpu.CompilerParams(dimension_semantics=sem, vmem_limit_bytes=VMEM_LIMIT_BYTES)


def _dot(a, b):
    return jnp.dot(a, b, preferred_element_type=F32)


def _dot_nt(a, b):
    return lax.dot_general(a, b, _NT, preferred_element_type=F32)


def _layer_norm(z, g, b):
    mu = jnp.mean(z, axis=-1, keepdims=True)
    zc = z - mu
    var = jnp.mean(zc * zc, axis=-1, keepdims=True)
    return zc * lax.rsqrt(var + LN_EPS) * g + b


def _proj_kernel(a_ref, w_ref, o_ref):
    o_ref[...] = _dot(a_ref[...].astype(BF16), w_ref[...]).astype(o_ref.dtype)


def _proj(a, w, tn, name):
    m, k = a.shape
    n = w.shape[1]
    return pl.pallas_call(
        _proj_kernel,
        grid=(n // tn,),
        in_specs=[pl.BlockSpec((m, k), lambda j: (0, 0)), pl.BlockSpec((k, tn), lambda j: (0, j))],
        out_specs=pl.BlockSpec((m, tn), lambda j: (0, j)),
        out_shape=jax.ShapeDtypeStruct((m, n), BF16),
        compiler_params=_params("parallel"),
        name=name,
    )(a, w)


def _proj_qkv_kernel(x_ref, wqk_ref, scale_ref, wvt_ref, qk_ref, vt_ref):
    xb = x_ref[...].astype(BF16)
    qk_ref[...] = (_dot(xb, wqk_ref[...]) * scale_ref[...]).astype(qk_ref.dtype)
    vt_ref[0] = _dot_nt(wvt_ref[...], xb).astype(vt_ref.dtype)


def _proj_qkv(x2d, w_qk, qk_scale, w_v_t, tk):
    m, k = x2d.shape
    n = w_qk.shape[1]
    nv = w_v_t.shape[0]
    return pl.pallas_call(
        _proj_qkv_kernel,
        grid=(m // tk,),
        in_specs=[
            pl.BlockSpec((tk, k), lambda j: (j, 0)),
            pl.BlockSpec((k, n), lambda j: (0, 0)),
            pl.BlockSpec((1, n), lambda j: (0, 0)),
            pl.BlockSpec((nv, k), lambda j: (0, 0)),
        ],
        out_specs=[pl.BlockSpec((tk, n), lambda j: (j, 0)), pl.BlockSpec((1, nv, tk), lambda j: (j, 0, 0))],
        out_shape=[jax.ShapeDtypeStruct((m, n), BF16), jax.ShapeDtypeStruct((m // tk, nv, tk), BF16)],
        compiler_params=_params("parallel"),
        name="proj_qkv",
    )(x2d, w_qk, qk_scale, w_v_t)


ATTN_ONES_ROWS = 16
ATTN_SKIP_LOG2 = 160.0
ALIBI_PIECES = 5


def _alibi_tables(t):
    slopes = np.exp2(-8.0 * np.arange(1, DA_HEADS + 1, dtype=np.float64) / DA_HEADS)
    ramp = float(LOG2E) * slopes[:, None] * np.arange(t, dtype=np.float64)[None, :]

    def pieces(v):
        out = []
        for _ in range(ALIBI_PIECES):
            piece = v.astype(np.float32).astype(BF16).astype(np.float64)
            out.append(piece)
            v = v - piece
        assert not v.any()
        return out

    ones = np.ones_like(ramp)
    pad = [np.zeros_like(ramp)] * (DA_V_DIM - 2 * ALIBI_PIECES)
    key_side = np.stack(pieces(ramp) + [ones] * ALIBI_PIECES + pad, axis=-1)
    query_side = np.stack([ones] * ALIBI_PIECES + pieces(-ramp) + pad, axis=-1)
    return jnp.asarray(key_side, BF16), jnp.asarray(query_side, BF16)


def _attn_kernel(slopes_ref, inv_tile_ref, lq1_ref, lk1_ref, lq2_ref, lk2_ref, sw_ref, q_ref, qb_ref, k_ref, kb_ref,
                 vt_ref, causal_ref, o_ref, sa_ref, sb_ref, m_ref, acc_ref, kmax_ref, *, t, heads, lam_init):
    hg = pl.program_id(1)
    i = pl.program_id(2)
    slope_log2 = [slopes_ref[hg * heads + g] * LOG2E for g in range(heads)]

    lane = lax.broadcasted_iota(jnp.int32, (t, DA_V_DIM), 1)
    q_maps = []
    for g in range(heads):
        q = q_ref[:, g * DA_V_DIM:(g + 1) * DA_V_DIM]
        zero = jnp.zeros_like(q)
        q_maps += [jnp.concatenate([jnp.where(lane < DA_QK_DIM, q, zero), qb_ref[g]], axis=1),
                   jnp.concatenate([jnp.where(lane >= DA_QK_DIM, q, zero), qb_ref[g]], axis=1)]
    ones_rows = jnp.ones((ATTN_ONES_ROWS, t), BF16)

    def scores(j, s_ref):
        rows = pl.ds(pl.multiple_of(j * t, t), t)
        for g in range(heads):
            k_aug = jnp.concatenate([k_ref[rows, g * DA_V_DIM:(g + 1) * DA_V_DIM], kb_ref[g]], axis=1)
            for mp in range(2):
                s_ref[2 * g + mp] = _dot_nt(k_aug, q_maps[2 * g + mp])

    def consume(j, s_ref, diagonal=False):
        for g in range(heads):
            vt_aug = jnp.concatenate([vt_ref[j, g * DA_V_DIM:(g + 1) * DA_V_DIM, :], ones_rows], axis=0)
            shift = slope_log2[g] * ((j - i) * t).astype(F32)
            for mp in range(2):
                c = 2 * g + mp
                s = s_ref[c]
                if diagonal:
                    s = s + causal_ref[...]
                    m_new = jnp.max(s, axis=0, keepdims=True)
                    acc_ref[c] = _dot(vt_aug, jnp.exp2(s - m_new).astype(BF16))
                    m_ref[c] = m_new
                    continue
                m_old = m_ref[c]
                m_new = jnp.maximum(m_old, jnp.max(s, axis=0, keepdims=True) + shift)
                p = jnp.exp2(s - (m_new - shift))
                acc_ref[c] = jnp.exp2(m_old - m_new) * acc_ref[c] + _dot(vt_aug, p.astype(BF16))
                m_ref[c] = m_new

    def largest_row_norm(rows):
        sq = jnp.square(rows.astype(F32))
        return jnp.max(jnp.sqrt(jnp.max(jnp.sum(sq, axis=1, keepdims=True), axis=0, keepdims=True)))

    @pl.when(i == 0)
    def _():
        for g in range(heads):
            kmax_ref[g] = largest_row_norm(k_ref[:, g * DA_V_DIM:(g + 1) * DA_V_DIM])

    scores(i, sa_ref)
    scores(jnp.maximum(i - 1, 0), sb_ref)
    consume(i, sa_ref, diagonal=True)

    keep = jnp.int32(0)
    for g in range(heads):
        m_min = jnp.min(m_ref[2 * g:2 * g + 2])
        q_max = largest_row_norm(q_ref[:, g * DA_V_DIM:(g + 1) * DA_V_DIM])
        reach = (ATTN_SKIP_LOG2 + q_max * kmax_ref[g] - m_min) * inv_tile_ref[hg * heads + g] + (t - 1) / t
        keep = jnp.maximum(keep, jnp.minimum(reach, float(2 ** 20)).astype(jnp.int32))
    keep = jnp.clip(keep, 0, i)

    def tile_pair(jj, carry):
        dist = 1 + 2 * jj
        scores(jnp.maximum(i - dist - 1, 0), sa_ref)
        consume(i - dist, sb_ref)
        scores(jnp.maximum(i - dist - 2, 0), sb_ref)
        consume(i - dist - 1, sa_ref)
        return carry

    lax.fori_loop(0, keep // 2, tile_pair, 0)

    @pl.when(keep % 2 == 1)
    def _():
        consume(i - keep, sb_ref)

    lam = (jnp.exp(jnp.sum(lq1_ref[...] * lk1_ref[...], axis=-1, keepdims=True))
           - jnp.exp(jnp.sum(lq2_ref[...] * lk2_ref[...], axis=-1, keepdims=True)) + lam_init)
    for g in range(heads):
        a1 = acc_ref[2 * g]
        a2 = acc_ref[2 * g + 1]
        o_t = (a1[:DA_V_DIM] / a1[DA_V_DIM:DA_V_DIM + 1]
               - lam * (a2[:DA_V_DIM] / a2[DA_V_DIM:DA_V_DIM + 1]))
        o = o_t.T
        o = o * lax.rsqrt(jnp.mean(o * o, axis=-1, keepdims=True) + LN_EPS)
        o_ref[:, g * DA_V_DIM:(g + 1) * DA_V_DIM] = (o * sw_ref[...] * (1.0 - lam_init)).astype(o_ref.dtype)


def _diff_attention(qk, vtb, slopes, lq1, lk1, lq2, lk2, subln_w, batch, seq, lam_init, t, heads):
    n_tok = qk.shape[0]
    nq = seq // t
    width = heads * DA_V_DIM
    key_bias, query_bias = _alibi_tables(t)
    pos = np.arange(t)
    causal = jnp.asarray(np.where(pos[:, None] <= pos[None, :], 0.0, NEG_INF), F32)
    vec = pl.BlockSpec((1, DA_QK_DIM), lambda b, h, i: (0, 0))
    bias_spec = pl.BlockSpec((heads, t, DA_V_DIM), lambda b, h, i: (h, 0, 0))
    maps = 2 * heads
    return pl.pallas_call(
        functools.partial(_attn_kernel, t=t, heads=heads, lam_init=lam_init),
        grid=(batch, DA_HEADS // heads, nq),
        in_specs=[
            pl.BlockSpec(memory_space=pltpu.SMEM),
            pl.BlockSpec(memory_space=pltpu.SMEM),
            vec, vec, vec, vec,
            pl.BlockSpec((1, DA_V_DIM), lambda b, h, i: (0, 0)),
            pl.BlockSpec((t, width), lambda b, h, i: (b * nq + i, h)),
            bias_spec,
            pl.BlockSpec((seq, width), lambda b, h, i: (b, DA_HEADS // heads + h)),
            bias_spec,
            pl.BlockSpec((nq, width, t), lambda b, h, i: (b, h, 0)),
            pl.BlockSpec((t, t), lambda b, h, i: (0, 0)),
        ],
        out_specs=pl.BlockSpec((t, width), lambda b, h, i: (b * nq + i, h)),
        out_shape=jax.ShapeDtypeStruct((n_tok, DA_WIDTH), BF16),
        scratch_shapes=[
            pltpu.VMEM((maps, t, t), F32),
            pltpu.VMEM((maps, t, t), F32),
            pltpu.VMEM((maps, 1, t), F32),
            pltpu.VMEM((maps, DA_V_DIM + ATTN_ONES_ROWS, t), F32),
            pltpu.SMEM((heads,), F32),
        ],
        compiler_params=_params("parallel", "parallel", "arbitrary"),
        name="diff_attention",
    )(slopes, 1.0 / (slopes * (LOG2E * t)), lq1, lk1, lq2, lk2, subln_w, qk, query_bias, qk, key_bias, vtb, causal)


def _mix_kernel(x_ref, xh_ref, o_ref, kv_ref, wp_ref, wgp_ref, wga_ref, poolw_ref, pscale_ref, wbp_ref, wba_ref,
                wout_ref, g_ref, b_ref, wq_ref, wo_ref, g2_ref, b2_ref, out_ref, e_ref, *, tm, seq):
    i = pl.program_id(0)
    start = (i * tm) % seq
    x = x_ref[...]
    xb = x.astype(BF16)

    p = _dot(xb, wp_ref[...])
    p_halo = _dot(xh_ref[...].astype(BF16), wp_ref[...])
    e_ref[0:POOL_HALO, :] = jnp.where(start == 0, 0.0, p_halo)
    e_ref[POOL_HALO:, :] = p

    pos = (start + 1 + lax.broadcasted_iota(jnp.int32, (tm, 1), 0)).astype(F32)
    ys = []
    for g, w in enumerate(POOL_WINDOWS):
        cols = slice(g * POOL_GROUP_DIM, (g + 1) * POOL_GROUP_DIM)
        acc = e_ref[POOL_HALO:POOL_HALO + tm, cols]
        for back in range(1, w):
            acc = acc + e_ref[POOL_HALO - back:POOL_HALO - back + tm, cols]
        pooled = acc / jnp.minimum(pos, float(w)) - e_ref[POOL_HALO:POOL_HALO + tm, cols]
        ys.append(_dot(pooled.astype(BF16), poolw_ref[g]))
    y = jnp.concatenate(ys, axis=-1) * pscale_ref[...]
    y_pool = _dot(y.astype(BF16), wbp_ref[...])

    gate_p = _dot(xb, wgp_ref[...])
    gate_a = _dot(xb, wga_ref[...])
    y_attn = _dot(o_ref[...], wba_ref[...])
    merged = jax.nn.sigmoid(gate_p) * y_pool + jax.nn.sigmoid(gate_a) * y_attn
    z = DN_ALPHA * x + _dot(merged.astype(BF16), wout_ref[...])
    x1 = _layer_norm(z, g_ref[...], b_ref[...])
    out_ref[...] = _xattn(x1, kv_ref, wq_ref, wo_ref, g2_ref, b2_ref)


def _mix_xattn(x2d, o, kvm, w_p, w_gp, w_ga, pool_w, pool_scale, w_bp, w_ba, w_out, g1, b1, w_cq, w_co, g2, b2, seq, tm):
    n_tok = x2d.shape[0]
    n_mem = kvm.shape[1]
    halo_blocks = tm // POOL_HALO

    def full(a):
        return pl.BlockSpec(a.shape, lambda i: (0,) * a.ndim, pipeline_mode=pl.Buffered(1))

    weights = (w_p, w_gp, w_ga, pool_w, pool_scale, w_bp, w_ba, w_out, g1, b1, w_cq, w_co, g2, b2)
    return pl.pallas_call(
        functools.partial(_mix_kernel, tm=tm, seq=seq),
        grid=(n_tok // tm,),
        in_specs=[
            pl.BlockSpec((tm, D_MODEL), lambda i: (i, 0)),
            pl.BlockSpec((POOL_HALO, D_MODEL), lambda i: (jnp.maximum(i * halo_blocks - 1, 0), 0)),
            pl.BlockSpec((tm, DA_WIDTH), lambda i: (i, 0)),
            pl.BlockSpec((1, n_mem, 2 * D_MODEL), lambda i: ((i * tm) // seq, 0, 0)),
        ] + [full(w) for w in weights],
        out_specs=pl.BlockSpec((tm, D_MODEL), lambda i: (i, 0)),
        out_shape=jax.ShapeDtypeStruct((n_tok, D_MODEL), F32),
        scratch_shapes=[pltpu.VMEM((tm + POOL_HALO, POOL_WIDTH), F32)],
        compiler_params=_params("parallel"),
        name="mix_xattn",
    )(x2d, x2d, o, kvm, *weights)


def _xattn(x, kv_ref, wq_ref, wo_ref, g_ref, b_ref):
    q = (_dot(x.astype(BF16), wq_ref[...]) * (XA_HEAD_DIM ** -0.5)).astype(BF16)
    kv = kv_ref[0]
    outs = []
    for h in range(XA_HEADS):
        cols = slice(h * XA_HEAD_DIM, (h + 1) * XA_HEAD_DIM)
        s = _dot_nt(q[:, cols], kv[:, cols])
        p = jnp.exp(s - jnp.max(s, axis=-1, keepdims=True))
        l = jnp.sum(p, axis=-1, keepdims=True)
        v = kv[:, D_MODEL + h * XA_HEAD_DIM:D_MODEL + (h + 1) * XA_HEAD_DIM]
        outs.append(_dot(p.astype(BF16), v) / l)
    o = jnp.concatenate(outs, axis=-1)
    z = DN_ALPHA * x + _dot(o.astype(BF16), wo_ref[...])
    return _layer_norm(z, g_ref[...], b_ref[...])


SUBLANES = 8


def _sort_network(n):
    def merge(lo, hi, r):
        step = 2 * r
        if step < hi - lo:
            yield from merge(lo, hi, step)
            yield from merge(lo + r, hi, step)
            yield from ((i, i + r) for i in range(lo + r, hi - r, step))
        else:
            yield (lo, lo + r)

    def sort(lo, hi):
        if hi - lo >= 1:
            mid = lo + (hi - lo) // 2
            yield from sort(lo, mid)
            yield from sort(mid + 1, hi)
            yield from merge(lo, hi, 1)

    return tuple(sort(0, n - 1))


def _exchange(v, i, j):
    if v[j] is None:
        return
    if v[i] is None:
        v[i], v[j] = v[j], None
        return
    v[i], v[j] = jnp.maximum(v[i], v[j]), jnp.minimum(v[i], v[j])


def _top_sorted(blocks, k):
    v = list(blocks) + [None] * (k - len(blocks))
    for i, j in _sort_network(k):
        _exchange(v, i, j)
    shift = SUBLANES // 2
    while shift >= 1:
        other = [None if a is None else pltpu.roll(a, shift, 0) for a in v]
        v = [b if a is None else (a if b is None else jnp.maximum(a, b)) for a, b in zip(v, reversed(other))]
        stride = k // 2
        while stride >= 1:
            for i in range(k):
                if not i & stride:
                    _exchange(v, i, i + stride)
            stride //= 2
        shift //= 2
    return v


def _route_kernel(x_ref, wq_ref, keys_ref, rank2_ref, e2_ref, count_ref, d_ref, *, heads):
    q_t = _dot_nt(wq_ref[...], x_ref[...].astype(BF16)).astype(BF16)
    for g in range(heads):
        q_g = q_t[g * PEER_QUERY_DIM:(g + 1) * PEER_QUERY_DIM]
        s1 = _dot(keys_ref[g, 0], q_g[:PEER_HALF])
        s2 = _dot(keys_ref[g, 1], q_g[PEER_HALF:])
        _route_head(s1, s2, g, rank2_ref, e2_ref, count_ref, d_ref)


def _route_head(s1, s2, g, rank2_ref, e2_ref, count_ref, d_ref):
    k = PEER_TOPK
    n_blocks = PEER_N_KEYS // SUBLANES
    s1_blocks = [s1[r * SUBLANES:(r + 1) * SUBLANES] for r in range(n_blocks)]
    s2_blocks = [s2[r * SUBLANES:(r + 1) * SUBLANES] for r in range(n_blocks)]
    v1 = _top_sorted(s1_blocks, k)
    v2 = _top_sorted(s2_blocks, k)

    row = lax.broadcasted_iota(jnp.int32, v1[0].shape, 0)

    def pack(vals):
        out = vals[0]
        for r in range(1, SUBLANES):
            out = jnp.where(row == r, vals[r], out)
        return out

    v1_lo, v1_hi, v2_hi = pack(v1[:SUBLANES]), pack(v1[SUBLANES:]), pack(v2[SUBLANES:])
    cand = ([v1_lo + v2[0], v1_hi + v2[0]] + [v1_lo + v2[b] for b in range(1, SUBLANES)] + [v2_hi + v1[0]])
    top = _top_sorted(cand, k)
    tau = top[k - 1]
    z = jnp.ones(tau.shape, F32)
    for c in top[1:]:
        z = z + jnp.exp(c - top[0])
    inv_z = 1.0 / z

    partners = []
    for a in range(k):
        n = jnp.zeros(tau.shape, F32)
        for b in range(k // (a + 1)):
            n = jnp.where(v1[a] + v2[b] >= tau, float(b + 1), n)
        partners.append(n)

    rank2, e2, count, d = [], [], [], []
    for blk1, blk2 in zip(s1_blocks, s2_blocks):
        r2 = jnp.full(blk2.shape, float(k), F32)
        cnt = jnp.zeros(blk1.shape, F32)
        for b in reversed(range(k)):
            r2 = jnp.where(blk2 >= v2[b], float(b), r2)
            cnt = jnp.where(blk1 >= v1[b], partners[b], cnt)
        rank2.append(r2)
        count.append(cnt)
        e2.append(jnp.exp(blk2 - v2[0]))
        d.append(jnp.exp(blk1 - v1[0]) * inv_z)
    rank2_ref[g] = jnp.concatenate(rank2, axis=0).astype(BF16)
    e2_ref[g] = jnp.concatenate(e2, axis=0).astype(BF16)
    count_ref[g] = jnp.concatenate(count, axis=0)
    d_ref[g] = jnp.concatenate(d, axis=0)


def _peer_route(x2, wq_t, keys, tm, heads):
    n_tok = x2.shape[0]
    out_spec = pl.BlockSpec((heads, PEER_N_KEYS, tm), lambda i, h: (h, 0, i))
    out_shape = [jax.ShapeDtypeStruct((PEER_HEADS, PEER_N_KEYS, n_tok), dt) for dt in (BF16, BF16, F32, F32)]
    return pl.pallas_call(
        functools.partial(_route_kernel, heads=heads),
        grid=(n_tok // tm, PEER_HEADS // heads),
        in_specs=[
            pl.BlockSpec((tm, D_MODEL), lambda i, h: (i, 0)),
            pl.BlockSpec((heads * PEER_QUERY_DIM, D_MODEL), lambda i, h: (h, 0)),
            pl.BlockSpec((heads, 2, PEER_N_KEYS, PEER_HALF), lambda i, h: (h, 0, 0, 0)),
        ],
        out_specs=[out_spec] * 4,
        out_shape=out_shape,
        compiler_params=_params("parallel", "parallel"),
        name="peer_route",
    )(x2, wq_t, keys)


def _peer_kernel(x_ref, u_ref, vt_ref, rank2_ref, e2_ref, count_ref, d_ref, g_ref, b_ref, out_ref,
                 xb_ref, acc_ref, a_ref, *, chunks, rows_per_chunk):
    e = pl.program_id(1)

    @pl.when(e == 0)
    def _():
        xb_ref[...] = (x_ref[...] * SQRT_HALF).astype(BF16)
        acc_ref[...] = jnp.zeros(acc_ref.shape, F32)

    xb = xb_ref[...]
    chunk = rows_per_chunk * PEER_N_KEYS
    tb = xb.shape[0]

    def sublane_rows(ref, h, row):
        tile = jnp.broadcast_to(ref[h, row:row + 1, :], (BF16_SUBLANES, tb)).astype(BF16)
        return jnp.concatenate([tile] * (PEER_N_KEYS // BF16_SUBLANES), axis=0)

    def activations(c):
        return _dot_nt(u_ref[c * chunk:(c + 1) * chunk, :], xb)

    a_ref[0] = activations(0)
    for c in range(chunks):
        if c + 1 < chunks:
            a_ref[(c + 1) % 2] = activations(c + 1)
        parts = []
        for r in range(rows_per_chunk):
            row = c * rows_per_chunk + r
            z = a_ref[c % 2, r * PEER_N_KEYS:(r + 1) * PEER_N_KEYS, :].astype(BF16)
            act = z + z * lax.erf(z)
            gate = None
            for h in range(PEER_HEADS):
                cnt = sublane_rows(count_ref, h, row)
                d = sublane_rows(d_ref, h, row)
                term = jnp.where(rank2_ref[h] < cnt, e2_ref[h], jnp.zeros((), BF16)) * d
                gate = term if gate is None else gate + term
            parts.append(gate * act)
        w = jnp.concatenate(parts, axis=0)
        acc_ref[...] += _dot(vt_ref[:, c * chunk:(c + 1) * chunk], w)

    @pl.when(e == pl.num_programs(1) - 1)
    def _():
        z = DN_ALPHA * x_ref[...] + SQRT_HALF * acc_ref[...].T
        out_ref[...] = _layer_norm(z, g_ref[...], b_ref[...])


def _peer_ln3(x2, u, vt, rank2, e2, count, d, g, b, tb, te, chunk):
    n_tok = x2.shape[0]
    rows_per_tile = te // PEER_N_KEYS
    key_spec = pl.BlockSpec((PEER_HEADS, PEER_N_KEYS, tb), lambda t, e: (0, 0, t))
    row_spec = pl.BlockSpec((PEER_HEADS, rows_per_tile, tb), lambda t, e: (0, e, t))
    vec = pl.BlockSpec((1, D_MODEL), lambda t, e: (0, 0))
    return pl.pallas_call(
        functools.partial(_peer_kernel, chunks=te // chunk, rows_per_chunk=chunk // PEER_N_KEYS),
        grid=(n_tok // tb, PEER_N_EXPERTS // te),
        in_specs=[
            pl.BlockSpec((tb, D_MODEL), lambda t, e: (t, 0)),
            pl.BlockSpec((te, D_MODEL), lambda t, e: (e, 0)),
            pl.BlockSpec((D_MODEL, te), lambda t, e: (0, e)),
            key_spec, key_spec, row_spec, row_spec,
            vec, vec,
        ],
        out_specs=pl.BlockSpec((tb, D_MODEL), lambda t, e: (t, 0)),
        out_shape=jax.ShapeDtypeStruct((n_tok, D_MODEL), F32),
        scratch_shapes=[
            pltpu.VMEM((tb, D_MODEL), BF16),
            pltpu.VMEM((D_MODEL, tb), F32),
            pltpu.VMEM((2, chunk, tb), F32),
        ],
        compiler_params=_params("parallel", "arbitrary"),
        name="peer_dense",
    )(x2, u, vt, rank2, e2, count, d, g, b)


def kernel(x, mem, w_in, pool_w, pool_scale, w_br_pool, lambda_q1, lambda_k1, lambda_q2, lambda_k2, subln_w, w_br_attn, w_out, ln1_g, ln1_b, w_cq, w_ckv, w_co, ln2_g, ln2_b, w_pq, sub_keys, expert_u, expert_v, ln3_g, ln3_b):
    batch, seq, _ = x.shape
    n_tok = batch * seq
    n_mem = mem.shape[1]
    tiles = TILES
    assert seq % tiles.attn == 0 and seq % tiles.mix == 0 and n_tok % tiles.peer_tokens == 0
    slopes = jnp.exp2(-8.0 * jnp.arange(1, DA_HEADS + 1, dtype=F32) / DA_HEADS)

    def row(v):
        return v.reshape(1, -1).astype(F32)

    for l in range(DEPTH):
        lam_init = 0.8 - 0.6 * math.exp(-0.3 * l)
        x2d = x.reshape(n_tok, D_MODEL)
        w_in_b = w_in[l].astype(BF16)
        q0 = POOL_WIDTH
        k0 = q0 + DA_WIDTH
        v0 = k0 + DA_WIDTH
        gp0 = v0 + DA_WIDTH
        ga0 = gp0 + D_MODEL

        qk_scale = jnp.concatenate([jnp.full((1, DA_WIDTH), LOG2E * DA_QK_DIM ** -0.5, F32),
                                    jnp.ones((1, DA_WIDTH), F32)], axis=1)
        qk, vtb = _proj_qkv(x2d, w_in_b[:, q0:v0], qk_scale, w_in_b[:, v0:gp0].T, tiles.attn)
        o = _diff_attention(qk, vtb, slopes, row(lambda_q1[l]), row(lambda_k1[l]), row(lambda_q2[l]),
                            row(lambda_k2[l]), row(subln_w[l]), batch, seq, lam_init, tiles.attn, tiles.attn_heads)

        kvm = _proj(mem.reshape(batch * n_mem, D_MODEL), w_ckv[l].astype(BF16), tiles.mem_cols,
                    "proj_mem_kv").reshape(batch, n_mem, 2 * D_MODEL)
        x2 = _mix_xattn(x2d, o, kvm, w_in_b[:, :POOL_WIDTH], w_in_b[:, gp0:ga0], w_in_b[:, ga0:],
                        pool_w[l].astype(BF16), row(pool_scale[l]), w_br_pool[l].astype(BF16),
                        w_br_attn[l].astype(BF16), w_out[l].astype(BF16), row(ln1_g[l]), row(ln1_b[l]),
                        w_cq[l].astype(BF16), w_co[l].astype(BF16), row(ln2_g[l]), row(ln2_b[l]), seq, tiles.mix)

        rank2, e2, count, d = _peer_route(x2, w_pq[l].T.astype(BF16), sub_keys[l].astype(BF16),
                                          tiles.peer_tokens, tiles.route_heads)
        x3 = _peer_ln3(x2, expert_u[l].astype(BF16), expert_v[l].T.astype(BF16), rank2, e2, count, d,
                       row(ln3_g[l]), row(ln3_b[l]), tiles.peer_tokens, tiles.peer_experts, tiles.peer_chunk)
        x = x3.reshape(batch, seq, D_MODEL)
    return x
```
